```python
import math
import jax, jax.numpy as jnp
from jax import lax
import numpy as np

D_MODEL = 2048
BATCH = 2
SEQ = 4096
DEPTH = 4
DEC_BATCH = 8
DEC_SEQ = 8
PAST_LEN = 16384
PAGE_SIZE = 128

HEAD_DIM = 128
N_HEADS = D_MODEL // HEAD_DIM
HGRN_HEADS = N_HEADS // 2
MOBA_HEADS = N_HEADS - HGRN_HEADS
HGRN_DK = HEAD_DIM
HGRN_DV = HEAD_DIM
HGRN_W = HGRN_HEADS * HGRN_DK
MOBA_W = MOBA_HEADS * HEAD_DIM
MIX_W = HGRN_HEADS * HGRN_DV + MOBA_W
MOBA_BLOCK = 256
MOBA_TOPK = 3
MOBA_Q_CHUNK = 16
GLA_HEADS = 4
GLA_DK = D_MODEL // 2 // GLA_HEADS
GLA_DV = D_MODEL // GLA_HEADS
GLA_KW = GLA_HEADS * GLA_DK
GLA_VW = GLA_HEADS * GLA_DV
GLA_GATE_RANK = 16
GLA_GATE_NORM = 16.0
D_FF = ((8 * D_MODEL // 3 + 255) // 256) * 256
REC_CHUNK = 16
N_EVEN = (DEPTH + 1) // 2
N_ODD = DEPTH // 2
EVEN_IN = 4 * HGRN_W + 3 * MOBA_W
ODD_IN = 2 * GLA_KW + 2 * GLA_VW + GLA_GATE_RANK
NORM_EPS = 1e-6

kernel_name = "hgrn2_moba_gla_macaron_step"


def rms_norm(x, w):
    xf = x.astype(jnp.float32)
    y = xf * lax.rsqrt(jnp.mean(xf * xf, axis=-1, keepdims=True) + NORM_EPS)
    return (y * w.astype(jnp.float32)).astype(x.dtype)


def swiglu(x, wg, wu, wd):
    return (jax.nn.silu(x @ wg) * (x @ wu)) @ wd


def gated_linear_recurrence(q, k, v, log_f, s0):
    B, L, H, dk = q.shape
    dv = v.shape[-1]
    c = REC_CHUNK
    n = -(-L // c)
    pad = n * c - L
    f32 = jnp.float32

    def prep(a):
        a = jnp.pad(a.astype(f32), ((0, 0), (0, pad), (0, 0), (0, 0)))
        return a.reshape(B, n, c, H, a.shape[-1]).transpose(1, 0, 3, 2, 4)

    causal = jnp.tril(jnp.ones((c, c), bool))[:, :, None]

    def step(S, inp):
        qb, kb, vb, gb = inp
        b = jnp.cumsum(gb, axis=2)
        b_last = b[:, :, -1:, :]
        inter = jnp.einsum('bhtk,bhkv->bhtv', qb * jnp.exp(b), S)
        diff = jnp.where(causal, b[:, :, :, None, :] - b[:, :, None, :, :], -jnp.inf)
        att = jnp.einsum('bhtk,bhsk,bhtsk->bhts', qb, kb, jnp.exp(diff))
        intra = jnp.einsum('bhts,bhsv->bhtv', att, vb)
        S_new = S * jnp.exp(b_last[:, :, 0, :])[..., None] + jnp.einsum('bhsk,bhsv->bhkv', kb * jnp.exp(b_last - b), vb)
        return S_new, inter + intra

    S, o = lax.scan(step, s0.astype(f32), (prep(q), prep(k), prep(v), prep(log_f)))
    o = o.transpose(1, 0, 3, 2, 4).reshape(B, n * c, H, dv)[:, :L]
    return o.astype(q.dtype), S.astype(q.dtype)


def hgrn2_mix(q_raw, f_raw, i_raw, g_raw, lb, norm_w, s0):
    B, L, _ = q_raw.shape
    heads = lambda a: a.reshape(B, L, HGRN_HEADS, -1)
    q = jax.nn.silu(heads(q_raw)) * (HGRN_DK ** -0.5)
    f = lb + (1.0 - lb) * jax.nn.sigmoid(f_raw.astype(jnp.float32))
    o, S = gated_linear_recurrence(q, heads(1.0 - f), heads(i_raw), heads(jnp.log(f)), s0)
    o = rms_norm(o, norm_w) * jax.nn.silu(heads(g_raw))
    return o.reshape(B, L, HGRN_HEADS * HGRN_DV), S


def moba_attend(q, k_all, v_all, q_pos0):
    B, Lq, H, dh = q.shape
    T = k_all.shape[1]
    nb = -(-T // MOBA_BLOCK)
    padT = nb * MOBA_BLOCK - T
    blocks = lambda a: jnp.pad(a, ((0, 0), (0, padT), (0, 0), (0, 0))).reshape(B, nb, MOBA_BLOCK, H, dh).transpose(0, 3, 1, 2, 4)
    kb, vb = blocks(k_all), blocks(v_all)
    k_mean = jnp.mean(kb.astype(jnp.float32), axis=3)
    topk = min(MOBA_TOPK, nb)
    qc = math.gcd(Lq, MOBA_Q_CHUNK)
    nq = Lq // qc
    q_chunks = q.reshape(B, nq, qc, H, dh).transpose(1, 0, 2, 3, 4)
    pos_chunks = (q_pos0 + jnp.arange(Lq, dtype=jnp.int32)).reshape(nq, qc)
    bi = jnp.arange(B)[:, None, None, None]
    hi = jnp.arange(H)[None, :, None, None]
    blk_ids = jnp.arange(nb, dtype=jnp.int32)
    scale = dh ** -0.5

    def one_chunk(args):
        qq, pos = args
        own = pos // MOBA_BLOCK
        scores = jnp.einsum('bqhd,bhnd->bhqn', qq.astype(jnp.float32), k_mean)
        past = blk_ids[None, :] < own[:, None]
        scores = jnp.where(past[None, None], scores, -jnp.inf)
        _, sel = lax.top_k(scores, topk)
        own_b = jnp.broadcast_to(own[None, None, :, None], (B, H, qc, 1))
        idx = jnp.concatenate([sel, own_b], axis=-1)
        valid = jnp.concatenate([sel < own_b, jnp.ones((B, H, qc, 1), bool)], axis=-1)
        kg = kb[bi, hi, idx]
        vg = vb[bi, hi, idx]
        key_pos = idx[..., None] * MOBA_BLOCK + jnp.arange(MOBA_BLOCK, dtype=jnp.int32)
        mask = valid[..., None] & (key_pos <= pos[None, None, :, None, None])
        logits = jnp.einsum('bqhd,bhqnkd->bhqnk', qq, kg).astype(jnp.float32) * scale
        logits = jnp.where(mask, logits, -jnp.inf)
        p = jax.nn.softmax(logits.reshape(B, H, qc, -1), axis=-1).reshape(logits.shape)
        return jnp.einsum('bhqnk,bhqnkd->bqhd', p.astype(vg.dtype), vg)

    out = lax.map(one_chunk, (q_chunks, pos_chunks))
    return out.transpose(1, 0, 2, 3, 4).reshape(B, Lq, H, dh)


def even_mixer(h, w_in, w_out, lb, hgrn_norm, s0, k_past, v_past, q_pos0):
    B, L, _ = h.shape
    proj = h @ w_in
    cuts = [HGRN_W, 2 * HGRN_W, 3 * HGRN_W, 4 * HGRN_W, 4 * HGRN_W + MOBA_W, 4 * HGRN_W + 2 * MOBA_W]
    hq, hf, hi_, hg, mq, mk, mv = jnp.split(proj, cuts, axis=-1)
    o_h, S = hgrn2_mix(hq, hf, hi_, hg, lb, hgrn_norm, s0)
    heads = lambda a: a.reshape(B, L, MOBA_HEADS, HEAD_DIM)
    k_new, v_new = heads(mk), heads(mv)
    k_all = k_new if k_past is None else jnp.concatenate([k_past, k_new], axis=1)
    v_all = v_new if v_past is None else jnp.concatenate([v_past, v_new], axis=1)
    o_m = moba_attend(heads(mq), k_all, v_all, q_pos0).reshape(B, L, MOBA_W)
    y = jnp.concatenate([o_h, o_m], axis=-1) @ w_out
    return y, S, k_new, v_new


def gla_mixer(h, w_in, w_gate_up, b_gate, norm_w, w_out, s0):
    B, L, _ = h.shape
    proj = h @ w_in
    q, k, v, r, a = jnp.split(proj, [GLA_KW, 2 * GLA_KW, 2 * GLA_KW + GLA_VW, 2 * GLA_KW + 2 * GLA_VW], axis=-1)
    log_a = jax.nn.log_sigmoid((a @ w_gate_up + b_gate).astype(jnp.float32)) / GLA_GATE_NORM
    hk = lambda t: t.reshape(B, L, GLA_HEADS, GLA_DK)
    hv = lambda t: t.reshape(B, L, GLA_HEADS, GLA_DV)
    o, S = gated_linear_recurrence(hk(q) * (GLA_DK ** -0.5), hk(k), hv(v), hk(log_a), s0)
    o = rms_norm(o, norm_w) * jax.nn.silu(hv(r))
    return o.reshape(B, L, GLA_VW) @ w_out, S


def run_trunk(x, hgrn_s0, gla_s0, k_past, v_past, q_pos0, lb, norm_w, ffn_gate, ffn_up, ffn_down,
              even_w_in, even_w_out, hgrn_norm_w, gla_w_in, gla_w_gate_up, gla_b_gate, gla_norm_w,
              gla_w_out, final_norm_w):
    ks, vs, hs, gs = [], [], [], []
    for l in range(DEPTH):
        x = x + 0.5 * swiglu(rms_norm(x, norm_w[l, 0]), ffn_gate[l, 0], ffn_up[l, 0], ffn_down[l, 0])
        h = rms_norm(x, norm_w[l, 1])
        if l % 2 == 0:
            e = l // 2
            kp = None if k_past is None else k_past[e]
            vp = None if v_past is None else v_past[e]
            y, S, kn, vn = even_mixer(h, even_w_in[e], even_w_out[e], lb[e], hgrn_norm_w[e], hgrn_s0[e], kp, vp, q_pos0)
            ks.append(kn)
            vs.append(vn)
            hs.append(S)
        else:
            o = l // 2
            y, S = gla_mixer(h, gla_w_in[o], gla_w_gate_up[o], gla_b_gate[o], gla_norm_w[o], gla_w_out[o], gla_s0[o])
            gs.append(S)
        x = x + y
        x = x + 0.5 * swiglu(rms_norm(x, norm_w[l, 2]), ffn_gate[l, 1], ffn_up[l, 1], ffn_down[l, 1])
    return rms_norm(x, final_norm_w), ks, vs, hs, gs


def setup_inputs(seed: int = 0) -> dict:
    key = jax.random.key(seed)
    ks = jax.random.split(key, 24)
    f32 = jnp.float32
    nrm = lambda k, shape, scale: jax.random.normal(k, shape, f32) * scale
    n_pages = PAST_LEN // PAGE_SIZE
    n_used = DEC_BATCH * n_pages
    n_pool = n_used + (n_used + 3) // 4
    page_table = jax.random.permutation(ks[6], n_pool)[:n_used].reshape(DEC_BATCH, n_pages).astype(jnp.int32)
    return {
        'x_prompt': nrm(ks[0], (BATCH, SEQ, D_MODEL), 1.0),
        'x_sample': nrm(ks[1], (DEC_BATCH, DEC_SEQ, D_MODEL), 1.0),
        'cache_k': nrm(ks[2], (N_EVEN, n_pool, PAGE_SIZE, MOBA_HEADS, HEAD_DIM), 1.0),
        'cache_v': nrm(ks[3], (N_EVEN, n_pool, PAGE_SIZE, MOBA_HEADS, HEAD_DIM), 1.0),
        'state_hgrn': nrm(ks[4], (N_EVEN, DEC_BATCH, HGRN_HEADS, HGRN_DK, HGRN_DV), 1.0),
        'state_gla': nrm(ks[5], (N_ODD, DEC_BATCH, GLA_HEADS, GLA_DK, GLA_DV), 1.0),
        'page_table': page_table,
        'norm_w': 1.0 + nrm(ks[7], (DEPTH, 3, D_MODEL), 0.02),
        'ffn_gate': nrm(ks[8], (DEPTH, 2, D_MODEL, D_FF), D_MODEL ** -0.5),
        'ffn_up': nrm(ks[9], (DEPTH, 2, D_MODEL, D_FF), D_MODEL ** -0.5),
        'ffn_down': nrm(ks[10], (DEPTH, 2, D_FF, D_MODEL), D_FF ** -0.5),
        'even_w_in': nrm(ks[11], (N_EVEN, D_MODEL, EVEN_IN), D_MODEL ** -0.5),
        'even_w_out': nrm(ks[12], (N_EVEN, MIX_W, D_MODEL), MIX_W ** -0.5),
        'hgrn_lb_logits': nrm(ks[13], (N_EVEN, HGRN_W), 0.5),
        'hgrn_norm_w': 1.0 + nrm(ks[14], (N_EVEN, HGRN_DV), 0.02),
        'gla_w_in': nrm(ks[15], (N_ODD, D_MODEL, ODD_IN), D_MODEL ** -0.5),
        'gla_w_gate_up': nrm(ks[16], (N_ODD, GLA_GATE_RANK, GLA_KW), GLA_GATE_RANK ** -0.5),
        'gla_b_gate': nrm(ks[17], (N_ODD, GLA_KW), 0.1),
        'gla_norm_w': 1.0 + nrm(ks[18], (N_ODD, GLA_DV), 0.02),
        'gla_w_out': nrm(ks[19], (N_ODD, GLA_VW, D_MODEL), GLA_VW ** -0.5),
        'final_norm_w': 1.0 + nrm(ks[20], (D_MODEL,), 0.02),
    }


def reference(x_prompt, x_sample, cache_k, cache_v, state_hgrn, state_gla, page_table, norm_w, ffn_gate,
              ffn_up, ffn_down, even_w_in, even_w_out, hgrn_lb_logits, hgrn_norm_w, gla_w_in, gla_w_gate_up,
              gla_b_gate, gla_norm_w, gla_w_out, final_norm_w):
    lb = jnp.cumsum(jax.nn.softmax(hgrn_lb_logits.astype(jnp.float32), axis=0), axis=0)
    lb = lb - lb[0:1]
    weights = (norm_w, ffn_gate, ffn_up, ffn_down, even_w_in, even_w_out, hgrn_norm_w, gla_w_in,
               gla_w_gate_up, gla_b_gate, gla_norm_w, gla_w_out, final_norm_w)

    Bp = x_prompt.shape[0]
    h0 = [jnp.zeros((Bp, HGRN_HEADS, HGRN_DK, HGRN_DV), x_prompt.dtype) for _ in range(N_EVEN)]
    g0 = [jnp.zeros((Bp, GLA_HEADS, GLA_DK, GLA_DV), x_prompt.dtype) for _ in range(N_ODD)]
    y_prompt, kp, vp, hp, gp = run_trunk(x_prompt, h0, g0, None, None, 0, lb, *weights)

    Bs = x_sample.shape[0]
    past_len = page_table.shape[1] * PAGE_SIZE
    k_past = [cache_k[e][page_table].reshape(Bs, past_len, MOBA_HEADS, HEAD_DIM) for e in range(N_EVEN)]
    v_past = [cache_v[e][page_table].reshape(Bs, past_len, MOBA_HEADS, HEAD_DIM) for e in range(N_EVEN)]
    hs0 = [state_hgrn[e] for e in range(N_EVEN)]
    gs0 = [state_gla[o] for o in range(N_ODD)]
    y_sample, ksn, vsn, hsn, gsn = run_trunk(x_sample, hs0, gs0, k_past, v_past, past_len, lb, *weights)

    k_prompt = jnp.stack([k.reshape(Bp, -1, PAGE_SIZE, MOBA_HEADS, HEAD_DIM) for k in kp])
    v_prompt = jnp.stack([v.reshape(Bp, -1, PAGE_SIZE, MOBA_HEADS, HEAD_DIM) for v in vp])
    hgrn_prompt = jnp.stack(hp)
    gla_prompt = jnp.stack(gp)
    k_sample = jnp.stack(ksn)
    v_sample = jnp.stack(vsn)
    hgrn_sample = jnp.stack(hsn)
    gla_sample = jnp.stack(gsn)
    return (y_prompt, y_sample, k_prompt, v_prompt, hgrn_prompt, gla_prompt, k_sample, v_sample, hgrn_sample, gla_sample)
```

```python
import functools

import numpy as np
import jax
import jax.numpy as jnp
from jax import lax
from jax.experimental import pallas as pl
from jax.experimental.pallas import tpu as pltpu

F32 = jnp.float32
BF16 = jnp.bfloat16

NORM_EPS = 1e-6
HEAD_DIM = 128
HGRN_HEADS = 8
MOBA_HEADS = 8
MOBA_BLOCK = 256
MOBA_TOPK = 3
PAGE_SIZE = 128
GLA_HEADS = 4
GLA_GATE_RANK = 16
GLA_GATE_NORM = 16.0

LANES = 128
REC_CHUNK = 128
REC_LEVELS = 7
VMEM_LIMIT = 56 * 1024 * 1024

NT_DIMS = (((1,), (1,)), ((), ()))
TN_DIMS = (((0,), (0,)), ((), ()))
NEG_INF = float("-inf")


def _params(*sem):
    return pltpu.CompilerParams(dimension_semantics=sem, vmem_limit_bytes=VMEM_LIMIT)


def _rms(x, w):
    return x * lax.rsqrt(jnp.mean(x * x, axis=-1, keepdims=True) + NORM_EPS) * w


def _silu(x):
    return x * jax.nn.sigmoid(x)


def _bdot(a, b):
    return jnp.dot(a.astype(BF16), b.astype(BF16), preferred_element_type=F32)


def _bdot_nt(a, b):
    return lax.dot_general(a.astype(BF16), b.astype(BF16), NT_DIMS, preferred_element_type=F32)


def _ffn_body(x_ref, nw_ref, wg_ref, wu_ref, wd_ref, *rest, final):
    if final:
        fw_ref, o_ref, h_ref = rest
    else:
        o_ref, h_ref = rest
    j = pl.program_id(1)

    @pl.when(j == 0)
    def _():
        h_ref[...] = _rms(x_ref[...], nw_ref[...]).astype(BF16)
        o_ref[...] = jnp.zeros_like(o_ref)

    h = h_ref[...]
    g = jnp.dot(h, wg_ref[...], preferred_element_type=F32)
    u = jnp.dot(h, wu_ref[...], preferred_element_type=F32)
    a = (_silu(g) * u).astype(BF16)
    o_ref[...] += jnp.dot(a, wd_ref[...], preferred_element_type=F32)

    @pl.when(j == pl.num_programs(1) - 1)
    def _():
        y = x_ref[...] + 0.5 * o_ref[...]
        if final:
            y = _rms(y, fw_ref[...])
        o_ref[...] = y


def ffn(x, nw, wg, wu, wd, final_w=None, *, tm, tf=512):
    M, D = x.shape
    FF = wg.shape[1]
    assert M % tm == 0 and FF % tf == 0
    final = final_w is not None
    in_specs = [
        pl.BlockSpec((tm, D), lambda i, j: (i, 0)),
        pl.BlockSpec((1, D), lambda i, j: (0, 0)),
        pl.BlockSpec((D, tf), lambda i, j: (0, j)),
        pl.BlockSpec((D, tf), lambda i, j: (0, j)),
        pl.BlockSpec((tf, D), lambda i, j: (j, 0)),
    ]
    args = [x, nw.reshape(1, D), wg, wu, wd]
    if final:
        in_specs.append(pl.BlockSpec((1, D), lambda i, j: (0, 0)))
        args.append(final_w.reshape(1, D))
    return pl.pallas_call(
        functools.partial(_ffn_body, final=final),
        grid=(M // tm, FF // tf),
        in_specs=in_specs,
        out_specs=pl.BlockSpec((tm, D), lambda i, j: (i, 0)),
        out_shape=jax.ShapeDtypeStruct((M, D), F32),
        scratch_shapes=[pltpu.VMEM((tm, D), BF16)],
        compiler_params=_params("parallel", "arbitrary"),
        name="ffn",
    )(*args)


def _norm_mm_body(x_ref, nw_ref, w_ref, o_ref, h_ref):
    @pl.when(pl.program_id(1) == 0)
    def _():
        h_ref[...] = _rms(x_ref[...], nw_ref[...]).astype(BF16)

    o_ref[...] = jnp.dot(h_ref[...], w_ref[...], preferred_element_type=F32)


def norm_matmul(x, nw, w, *, tm, tn):
    M, D = x.shape
    N = w.shape[1]
    assert M % tm == 0 and N % tn == 0
    return pl.pallas_call(
        _norm_mm_body,
        grid=(M // tm, N // tn),
        in_specs=[
            pl.BlockSpec((tm, D), lambda i, j: (i, 0)),
            pl.BlockSpec((1, D), lambda i, j: (0, 0)),
            pl.BlockSpec((D, tn), lambda i, j: (0, j)),
        ],
        out_specs=pl.BlockSpec((tm, tn), lambda i, j: (i, j)),
        out_shape=jax.ShapeDtypeStruct((M, N), F32),
        scratch_shapes=[pltpu.VMEM((tm, D), BF16)],
        compiler_params=_params("parallel", "arbitrary"),
        name="norm_matmul",
    )(x, nw.reshape(1, D), w)


def _out_proj_body(x_ref, a_ref, b_ref, wa_ref, wb_ref, o_ref):
    o_ref[...] = x_ref[...] + _bdot(a_ref[...], wa_ref[...]) + _bdot(b_ref[...], wb_ref[...])


def out_proj(x, a, a_blk, b, b_blk, w, *, tm):
    M, D = x.shape
    K = w.shape[0] // 2
    assert M % tm == 0
    return pl.pallas_call(
        _out_proj_body,
        grid=(M // tm,),
        in_specs=[
            pl.BlockSpec((tm, D), lambda i: (i, 0)),
            pl.BlockSpec((tm, K), lambda i: (i, a_blk)),
            pl.BlockSpec((tm, K), lambda i: (i, b_blk)),
            pl.BlockSpec((K, D), lambda i: (0, 0)),
            pl.BlockSpec((K, D), lambda i: (1, 0)),
        ],
        out_specs=pl.BlockSpec((tm, D), lambda i: (i, 0)),
        out_shape=jax.ShapeDtypeStruct((M, D), F32),
        compiler_params=_params("parallel"),
        name="out_proj",
    )(x, a, b, w, w)


def _rec_constants():
    C = REC_CHUNK
    t = np.arange(C)[:, None]
    s = np.arange(C)[None, :]
    mats = [(s <= t).astype(np.float32)]
    a_mats, b_mats = [], []
    level = np.full((C, C), -1, np.int32)
    level[np.arange(C), np.arange(C)] = REC_LEVELS
    for li in range(REC_LEVELS):
        L = C >> (li + 1)
        mid = (t // (2 * L)) * (2 * L) + L
        a_mats.append(((t >= mid) & (s >= mid) & (s <= t)).astype(np.float32))
        b_mats.append(((t < mid) & (s >= t + 1) & (s <= mid - 1)).astype(np.float32))
        mid_s = (s // (2 * L)) * (2 * L) + L
        level[(t >= mid) & (s < mid_s) & (mid_s == mid)] = li
    w = np.concatenate(mats + a_mats + b_mats, axis=0)
    return jnp.asarray(w, BF16), jnp.asarray(level)


def _rec_core(q, k, v, g, w_ref, level, st_ref):
    C = REC_CHUNK
    g1 = g.astype(BF16)
    r1 = g - g1.astype(F32)
    g2 = r1.astype(BF16)
    g3 = (r1 - g2.astype(F32)).astype(BF16)
    w = w_ref[...]
    e = (jnp.dot(w, g1, preferred_element_type=F32) + jnp.dot(w, g2, preferred_element_type=F32)
         + jnp.dot(w, g3, preferred_element_type=F32))
    b = e[0:C]
    b_last = b[C - 1:C, :]
    st = st_ref[...]
    inter = _bdot_nt(q * jnp.exp(b), st)
    att = jnp.where(level == REC_LEVELS, _bdot_nt(q, k), 0.0)
    for li in range(REC_LEVELS):
        eq = e[(1 + li) * C:(2 + li) * C]
        ek = e[(1 + REC_LEVELS + li) * C:(2 + REC_LEVELS + li) * C]
        att = att + jnp.where(level == li, _bdot_nt(q * jnp.exp(eq), k * jnp.exp(ek)), 0.0)
    o = inter + _bdot(att, v)
    kd = k * jnp.exp(b_last - b)
    upd = lax.dot_general(v.astype(BF16), kd.astype(BF16), TN_DIMS, preferred_element_type=F32)
    st_ref[...] = st * jnp.exp(b_last) + upd
    return o


def _valid_rows(shape, valid_len):
    return lax.broadcasted_iota(jnp.int32, shape, 0) < valid_len


def _hgrn_body(hq_ref, hf_ref, hi_ref, hg_ref, lb_ref, nw_ref, s0_ref, w_ref, lvl_ref, o_ref, so_ref, st_ref,
               *, valid_len):
    c = pl.program_id(2)

    @pl.when(c == 0)
    def _():
        st_ref[...] = s0_ref[...].T

    dk = hq_ref.shape[-1]
    q = _silu(hq_ref[...]) * (dk ** -0.5)
    lb = lb_ref[...]
    f = lb + (1.0 - lb) * jax.nn.sigmoid(hf_ref[...])
    k = 1.0 - f
    g = jnp.log(f)
    if valid_len < REC_CHUNK:
        ok = _valid_rows(g.shape, valid_len)
        k = jnp.where(ok, k, 0.0)
        g = jnp.where(ok, g, 0.0)
    o = _rec_core(q, k, hi_ref[...], g, w_ref, lvl_ref[...], st_ref)
    o_ref[...] = _rms(o, nw_ref[...]) * _silu(hg_ref[...])

    @pl.when(c == pl.num_programs(2) - 1)
    def _():
        so_ref[...] = st_ref[...].T


def hgrn_mix(proj, lb, norm_w, s0, *, batch, valid_len=REC_CHUNK):
    H, dk, dv = s0.shape[1:]
    C = REC_CHUNK
    n = proj.shape[0] // (batch * C)
    wmat, level = _rec_constants()
    row = lambda b, h, c: b * n + c
    col_spec = lambda grp: pl.BlockSpec((C, dk), lambda b, h, c: (row(b, h, c), grp * H + h))
    const = lambda shape: pl.BlockSpec(shape, lambda b, h, c: (0,) * len(shape))
    state_spec = pl.BlockSpec((None, None, dk, dv), lambda b, h, c: (b, h, 0, 0))
    return pl.pallas_call(
        functools.partial(_hgrn_body, valid_len=valid_len),
        grid=(batch, H, n),
        in_specs=[col_spec(0), col_spec(1), col_spec(2), col_spec(3),
                  pl.BlockSpec((1, dk), lambda b, h, c: (0, h)),
                  const((1, dv)), state_spec, const(wmat.shape), const(level.shape)],
        out_specs=[pl.BlockSpec((C, dv), lambda b, h, c: (row(b, h, c), h)), state_spec],
        out_shape=[jax.ShapeDtypeStruct((batch * n * C, H * dv), F32), jax.ShapeDtypeStruct(s0.shape, F32)],
        scratch_shapes=[pltpu.VMEM((dv, dk), F32)],
        compiler_params=_params("parallel", "parallel", "arbitrary"),
        name="hgrn_mix",
    )(proj, proj, proj, proj, lb.reshape(1, H * dk), norm_w.reshape(1, dv), s0, wmat, level)


def _gla_body(q_ref, k_ref, v_ref, r_ref, a_ref, wg_ref, bg_ref, nw_ref, s0_ref, w_ref, lvl_ref, o_ref, so_ref,
              st_ref, *, valid_len):
    c = pl.program_id(2)

    @pl.when(c == 0)
    def _():
        st_ref[...] = s0_ref[...].T

    dk = q_ref.shape[-1]
    q = q_ref[...] * (dk ** -0.5)
    k = k_ref[...]
    x = _bdot(a_ref[...], wg_ref[...]) + bg_ref[...]
    g = -(jnp.maximum(-x, 0.0) + jnp.log(1.0 + jnp.exp(-jnp.abs(x)))) / GLA_GATE_NORM
    if valid_len < REC_CHUNK:
        ok = _valid_rows(g.shape, valid_len)
        k = jnp.where(ok, k, 0.0)
        g = jnp.where(ok, g, 0.0)
    o = _rec_core(q, k, v_ref[...], g, w_ref, lvl_ref[...], st_ref)
    o_ref[...] = _rms(o, nw_ref[...]) * _silu(r_ref[...])

    @pl.when(c == pl.num_programs(2) - 1)
    def _():
        so_ref[...] = st_ref[...].T


def gla_mix(proj, w_gate, b_gate, norm_w, s0, *, batch, valid_len=REC_CHUNK):
    H, dk, dv = s0.shape[1:]
    C = REC_CHUNK
    n = proj.shape[0] // (batch * C)
    wmat, level = _rec_constants()
    row = lambda b, h, c: b * n + c
    const = lambda shape: pl.BlockSpec(shape, lambda b, h, c: (0,) * len(shape))
    state_spec = pl.BlockSpec((None, None, dk, dv), lambda b, h, c: (b, h, 0, 0))
    kv_off = 2 * H * dk // dv
    a_blk = (2 * H * dk + 2 * H * dv) // LANES
    return pl.pallas_call(
        functools.partial(_gla_body, valid_len=valid_len),
        grid=(batch, H, n),
        in_specs=[pl.BlockSpec((C, dk), lambda b, h, c: (row(b, h, c), h)),
                  pl.BlockSpec((C, dk), lambda b, h, c: (row(b, h, c), H + h)),
                  pl.BlockSpec((C, dv), lambda b, h, c: (row(b, h, c), kv_off + h)),
                  pl.BlockSpec((C, dv), lambda b, h, c: (row(b, h, c), kv_off + H + h)),
                  pl.BlockSpec((C, LANES), lambda b, h, c: (row(b, h, c), a_blk)),
                  pl.BlockSpec((LANES, dk), lambda b, h, c: (0, h)),
                  pl.BlockSpec((1, dk), lambda b, h, c: (0, h)),
                  const((1, dv)), state_spec, const(wmat.shape), const(level.shape)],
        out_specs=[pl.BlockSpec((C, dv), lambda b, h, c: (row(b, h, c), h)), state_spec],
        out_shape=[jax.ShapeDtypeStruct((batch * n * C, H * dv), F32), jax.ShapeDtypeStruct(s0.shape, F32)],
        scratch_shapes=[pltpu.VMEM((dv, dk), F32)],
        compiler_params=_params("parallel", "parallel", "arbitrary"),
        name="gla_mix",
    )(proj, proj, proj, proj, proj, w_gate, b_gate.reshape(1, H * dk), norm_w.reshape(1, dv), s0, wmat, level)


def _top_blocks(s, lane):
    sel = jnp.zeros(s.shape, F32)
    for _ in range(MOBA_TOPK):
        m = jnp.max(s, axis=1, keepdims=True)
        idx = jnp.min(jnp.where(s == m, lane, s.shape[1]), axis=1, keepdims=True)
        hit = lane == idx
        sel = jnp.where(hit & (m > NEG_INF), 1.0, sel)
        s = jnp.where(hit, NEG_INF, s)
    return sel


def _moba_prompt_body(q_ref, k_ref, v_ref, o_ref, kmean_ref):
    i = pl.program_id(2)
    blk = MOBA_BLOCK
    nb = k_ref.shape[0] // blk
    dh = q_ref.shape[-1]

    @pl.when(i == 0)
    def _():
        kmean_ref[...] = jnp.zeros_like(kmean_ref)
        for jb in range(nb):
            kmean_ref[jb:jb + 1, :] = jnp.mean(k_ref[jb * blk:(jb + 1) * blk, :], axis=0, keepdims=True)

    q = q_ref[...]
    lane = lax.broadcasted_iota(jnp.int32, (blk, LANES), 1)
    s = lax.dot_general(q, kmean_ref[...], NT_DIMS, precision=lax.Precision.HIGHEST, preferred_element_type=F32)
    sel = _top_blocks(jnp.where(lane < i, s, NEG_INF), lane)

    qs = (q * (dh ** -0.5)).astype(BF16)
    own = pl.multiple_of(i * blk, blk)
    logits = _bdot_nt(qs, k_ref[pl.ds(own, blk), :])
    causal = lax.broadcasted_iota(jnp.int32, (blk, blk), 0) >= lax.broadcasted_iota(jnp.int32, (blk, blk), 1)
    logits = jnp.where(causal, logits, NEG_INF)
    m0 = jnp.max(logits, axis=1, keepdims=True)
    p0 = jnp.exp(logits - m0)
    l0 = jnp.sum(p0, axis=1, keepdims=True)
    acc0 = _bdot(p0, v_ref[pl.ds(own, blk), :])

    def body(j, carry):
        m, l, acc = carry
        picked = jnp.max(jnp.where(lane == j, sel, 0.0), axis=1, keepdims=True) > 0.0
        start = pl.multiple_of(j * blk, blk)
        lg = jnp.where(picked, _bdot_nt(qs, k_ref[pl.ds(start, blk), :]), NEG_INF)
        m_new = jnp.maximum(m, jnp.max(lg, axis=1, keepdims=True))
        alpha = jnp.exp(m - m_new)
        p = jnp.exp(lg - m_new)
        l = alpha * l + jnp.sum(p, axis=1, keepdims=True)
        acc = alpha * acc + _bdot(p, v_ref[pl.ds(start, blk), :])
        return m_new, l, acc

    _, l, acc = lax.fori_loop(0, i, body, (m0, l0, acc0))
    o_ref[...] = acc / l


def moba_prompt(proj, *, batch, q_col, k_col, v_col):
    H, dh, blk = MOBA_HEADS, HEAD_DIM, MOBA_BLOCK
    T = proj.shape[0] // batch
    assert T % blk == 0 and T // blk <= LANES
    nq = T // blk
    return pl.pallas_call(
        _moba_prompt_body,
        grid=(batch, H, nq),
        in_specs=[pl.BlockSpec((blk, dh), lambda b, h, i: (b * nq + i, q_col + h)),
                  pl.BlockSpec((T, dh), lambda b, h, i: (b, k_col + h)),
                  pl.BlockSpec((T, dh), lambda b, h, i: (b, v_col + h))],
        out_specs=pl.BlockSpec((blk, dh), lambda b, h, i: (b * nq + i, h)),
        out_shape=jax.ShapeDtypeStruct((batch * T, H * dh), F32),
        scratch_shapes=[pltpu.VMEM((LANES, dh), F32)],
        compiler_params=_params("parallel", "parallel", "arbitrary"),
        name="moba_prompt",
    )(proj, proj, proj)


def _stack_heads(q):
    heads = q.shape[1] // HEAD_DIM
    return jnp.concatenate([q[:, h * HEAD_DIM:(h + 1) * HEAD_DIM] for h in range(heads)], axis=0)


def _same_head(rows, cols, lq, heads):
    r = lax.broadcasted_iota(jnp.int32, (rows, cols), 0)
    c = lax.broadcasted_iota(jnp.int32, (rows, cols), 1)
    return (r // lq) == (c % heads)


def _moba_pages_body(pt_ref, q_ref, k_ref, v_ref, sc_ref, m_ref, l_ref, o_ref):
    p = pl.program_id(1)
    lq = q_ref.shape[0]
    n_tok, heads, dh = k_ref.shape

    @pl.when(p == 0)
    def _():
        sc_ref[...] = jnp.zeros_like(sc_ref)
        m_ref[...] = jnp.zeros_like(m_ref)
        l_ref[...] = jnp.zeros_like(l_ref)

    qa = _stack_heads(q_ref[...])
    rows = qa.shape[0]
    kp = k_ref[...]
    ksum = jnp.sum(kp, axis=0)
    ksum_rows = jnp.concatenate([jnp.broadcast_to(ksum[h:h + 1, :], (lq, dh)) for h in range(heads)], axis=0)
    sc = jnp.sum(qa * ksum_rows, axis=1, keepdims=True)
    logits = _bdot_nt(qa * (dh ** -0.5), kp.reshape(n_tok * heads, dh))
    logits = jnp.where(_same_head(rows, n_tok * heads, lq, heads), logits, NEG_INF)
    m = jnp.max(logits, axis=1, keepdims=True)
    pr = jnp.exp(logits - m)
    l = jnp.sum(pr, axis=1, keepdims=True)
    o_ref[...] = _bdot(pr, v_ref[...].reshape(n_tok * heads, dh))
    here = lax.broadcasted_iota(jnp.int32, sc_ref.shape, 1) == p
    sc_ref[...] = jnp.where(here, sc, sc_ref[...])
    m_ref[...] = jnp.where(here, m, m_ref[...])
    l_ref[...] = jnp.where(here, l, l_ref[...])


def moba_pages(q, cache_k, cache_v, page_table, layer):
    B, lq, width = q.shape
    _, _, n_tok, heads, dh = cache_k.shape
    n_pages = page_table.shape[1]
    assert n_pages == LANES and heads * dh == width
    rows = heads * lq
    page_spec = pl.BlockSpec((None, None, n_tok, heads, dh), lambda b, p, pt: (layer, pt[b, p], 0, 0, 0))
    stat_spec = pl.BlockSpec((None, rows, LANES), lambda b, p, pt: (b, 0, 0))
    stat = jax.ShapeDtypeStruct((B, rows, LANES), F32)
    return pl.pallas_call(
        _moba_pages_body,
        grid_spec=pltpu.PrefetchScalarGridSpec(
            num_scalar_prefetch=1,
            grid=(B, n_pages),
            in_specs=[pl.BlockSpec((None, lq, width), lambda b, p, pt: (b, 0, 0)), page_spec, page_spec],
            out_specs=[stat_spec, stat_spec, stat_spec,
                       pl.BlockSpec((None, None, rows, dh), lambda b, p, pt: (b, p, 0, 0))],
        ),
        out_shape=[stat, stat, stat, jax.ShapeDtypeStruct((B, n_pages, rows, dh), F32)],
        compiler_params=_params("parallel", "arbitrary"),
        name="moba_pages",
    )(page_table, q, cache_k, cache_v)


def _moba_combine_body(sc_ref, m_ref, l_ref, op_ref, q_ref, kn_ref, vn_ref, o_ref):
    lq = q_ref.shape[0]
    rows = sc_ref.shape[0]
    heads = rows // lq
    n_pages = op_ref.shape[0]
    pages_per_block = MOBA_BLOCK // PAGE_SIZE
    assert pages_per_block == 2
    lane = lax.broadcasted_iota(jnp.int32, (rows, LANES), 1)
    sc = sc_ref[...]
    bs = (sc + pltpu.roll(sc, LANES - 1, 1)) * (1.0 / MOBA_BLOCK)
    sel = _top_blocks(jnp.where(lane % pages_per_block == 0, bs, NEG_INF), lane)
    sel = sel + pltpu.roll(sel, 1, 1)

    qa = _stack_heads(q_ref[...])
    lo = _bdot_nt(qa * (HEAD_DIM ** -0.5), kn_ref[...])
    t_q = lax.broadcasted_iota(jnp.int32, (rows, LANES), 0) % lq
    lo = jnp.where(_same_head(rows, LANES, lq, heads) & (lane // heads <= t_q), lo, NEG_INF)
    m = m_ref[...]
    m_all = jnp.maximum(jnp.max(lo, axis=1, keepdims=True),
                        jnp.max(jnp.where(sel > 0.0, m, NEG_INF), axis=1, keepdims=True))
    w = jnp.where(sel > 0.0, jnp.exp(m - m_all), 0.0)
    p_own = jnp.exp(lo - m_all)
    den = jnp.sum(w * l_ref[...], axis=1, keepdims=True) + jnp.sum(p_own, axis=1, keepdims=True)
    num0 = _bdot(p_own, vn_ref[...])

    def body(p, num):
        wp = jnp.sum(jnp.where(lane == p, w, 0.0), axis=1, keepdims=True)
        return num + wp * op_ref[p]

    out = lax.fori_loop(0, n_pages, body, num0) / den
    o_ref[...] = jnp.concatenate([out[h * lq:(h + 1) * lq, :] for h in range(heads)], axis=1)


def moba_combine(sc, m, l, o_pages, q, k_new, v_new):
    B, lq, width = q.shape
    heads, dh = k_new.shape[2:]
    rows = sc.shape[1]
    n_pages = o_pages.shape[1]
    assert LANES % heads == 0 and lq * heads <= LANES
    pad = ((0, 0), (0, LANES // heads - lq), (0, 0), (0, 0))
    kn = jnp.pad(k_new, pad).reshape(B, LANES, dh)
    vn = jnp.pad(v_new, pad).reshape(B, LANES, dh)
    stat_spec = pl.BlockSpec((None, rows, LANES), lambda b: (b, 0, 0))
    new_spec = pl.BlockSpec((None, LANES, dh), lambda b: (b, 0, 0))
    q_spec = pl.BlockSpec((None, lq, width), lambda b: (b, 0, 0))
    return pl.pallas_call(
        _moba_combine_body,
        grid=(B,),
        in_specs=[stat_spec, stat_spec, stat_spec,
                  pl.BlockSpec((None, n_pages, rows, dh), lambda b: (b, 0, 0, 0)),
                  q_spec, new_spec, new_spec],
        out_specs=q_spec,
        out_shape=jax.ShapeDtypeStruct((B, lq, width), F32),
        compiler_params=_params("parallel"),
        name="moba_combine",
    )(sc, m, l, o_pages, q, kn, vn)


def _pad_tokens(a, batch, length):
    a = a.reshape(batch, -1, a.shape[-1])
    return jnp.pad(a, ((0, 0), (0, length - a.shape[1]), (0, 0))).reshape(batch * length, a.shape[-1])


def _unpad_tokens(a, batch, seq):
    return a.reshape(batch, -1, a.shape[-1])[:, :seq].reshape(batch * seq, a.shape[-1])


def _trunk(x, batch, hgrn_s0, gla_s0, past, lb, wts, *, tm):
    (norm_w, ffn_gate, ffn_up, ffn_down, even_w_in, even_w_out, hgrn_norm_w, gla_w_in, gla_w_gate,
     gla_b_gate, gla_norm_w, gla_w_out, final_norm_w) = wts
    depth = norm_w.shape[0]
    seq = x.shape[0] // batch
    padded = seq % REC_CHUNK != 0
    hw = HGRN_HEADS * HEAD_DIM
    mw = MOBA_HEADS * HEAD_DIM
    ks, vs, hs, gs = [], [], [], []
    for l in range(depth):
        x = ffn(x, norm_w[l, 0], ffn_gate[l, 0], ffn_up[l, 0], ffn_down[l, 0], tm=tm)
        if l % 2 == 0:
            e = l // 2
            proj = norm_matmul(x, norm_w[l, 1], even_w_in[e], tm=tm, tn=896)
            rec_in = _pad_tokens(proj[:, :4 * hw], batch, REC_CHUNK) if padded else proj
            o_h, s_h = hgrn_mix(rec_in, lb[e], hgrn_norm_w[e], hgrn_s0[e], batch=batch,
                                valid_len=seq if padded else REC_CHUNK)
            if padded:
                o_h = _unpad_tokens(o_h, batch, seq)
            k_new = proj[:, 4 * hw + mw:4 * hw + 2 * mw]
            v_new = proj[:, 4 * hw + 2 * mw:]
            if past is None:
                first = 4 * hw // HEAD_DIM
                o_m = moba_prompt(proj, batch=batch, q_col=first, k_col=first + MOBA_HEADS,
                                  v_col=first + 2 * MOBA_HEADS)
            else:
                cache_k, cache_v, page_table = past
                q = proj[:, 4 * hw:4 * hw + mw].reshape(batch, seq, mw)
                sc, m, lsum, o_pages = moba_pages(q, cache_k, cache_v, page_table, e)
                heads = lambda a: a.reshape(batch, seq, MOBA_HEADS, HEAD_DIM)
                o_m = moba_combine(sc, m, lsum, o_pages, q, heads(k_new), heads(v_new)).reshape(batch * seq, mw)
            x = out_proj(x, o_h, 0, o_m, 0, even_w_out[e], tm=tm)
            ks.append(k_new)
            vs.append(v_new)
            hs.append(s_h)
        else:
            o = l // 2
            proj = norm_matmul(x, norm_w[l, 1], gla_w_in[o], tm=tm, tn=896)
            rec_in = _pad_tokens(proj, batch, REC_CHUNK) if padded else proj
            o_g, s_g = gla_mix(rec_in, gla_w_gate[o], gla_b_gate[o], gla_norm_w[o], gla_s0[o], batch=batch,
                               valid_len=seq if padded else REC_CHUNK)
            if padded:
                o_g = _unpad_tokens(o_g, batch, seq)
            x = out_proj(x, o_g, 0, o_g, 1, gla_w_out[o], tm=tm)
            gs.append(s_g)
        x = ffn(x, norm_w[l, 2], ffn_gate[l, 1], ffn_up[l, 1], ffn_down[l, 1],
                final_norm_w if l == depth - 1 else None, tm=tm)
    return x, ks, vs, hs, gs


def kernel(x_prompt, x_sample, cache_k, cache_v, state_hgrn, state_gla, page_table, norm_w, ffn_gate, ffn_up,
           ffn_down, even_w_in, even_w_out, hgrn_lb_logits, hgrn_norm_w, gla_w_in, gla_w_gate_up, gla_b_gate,
           gla_norm_w, gla_w_out, final_norm_w):
    lb = jnp.cumsum(jax.nn.softmax(hgrn_lb_logits.astype(F32), axis=0), axis=0)
    lb = lb - lb[0:1]

    n_odd, d_model, odd_in = gla_w_in.shape
    gate_col = odd_in - GLA_GATE_RANK
    gla_w_in_p = jnp.pad(gla_w_in, ((0, 0), (0, 0), (0, LANES - GLA_GATE_RANK))).astype(BF16)
    gla_w_gate_p = jnp.pad(gla_w_gate_up, ((0, 0), (0, LANES - GLA_GATE_RANK), (0, 0))).astype(BF16)
    assert gate_col % LANES == 0
    wts = (norm_w, ffn_gate.astype(BF16), ffn_up.astype(BF16), ffn_down.astype(BF16), even_w_in.astype(BF16),
           even_w_out.astype(BF16), hgrn_norm_w, gla_w_in_p, gla_w_gate_p, gla_b_gate, gla_norm_w,
           gla_w_out.astype(BF16), final_norm_w)

    Bp, Lp, D = x_prompt.shape
    n_even = state_hgrn.shape[0]
    h0 = jnp.zeros((n_even, Bp) + state_hgrn.shape[2:], F32)
    g0 = jnp.zeros((n_odd, Bp) + state_gla.shape[2:], F32)
    y_p, kp, vp, hp, gp = _trunk(x_prompt.reshape(Bp * Lp, D), Bp, h0, g0, None, lb, wts, tm=512)

    Bs, Ls, _ = x_sample.shape
    y_s, ksn, vsn, hsn, gsn = _trunk(x_sample.reshape(Bs * Ls, D), Bs, state_hgrn, state_gla,
                                     (cache_k, cache_v, page_table), lb, wts, tm=Bs * Ls)

    pages = lambda a: a.reshape(Bp, Lp // PAGE_SIZE, PAGE_SIZE, MOBA_HEADS, HEAD_DIM)
    rows = lambda a: a.reshape(Bs, Ls, MOBA_HEADS, HEAD_DIM)
    return (y_p.reshape(Bp, Lp, D), y_s.reshape(Bs, Ls, D),
            jnp.stack([pages(a) for a in kp]), jnp.stack([pages(a) for a in vp]),
            jnp.stack(hp), jnp.stack(gp),
            jnp.stack([rows(a) for a in ksn]), jnp.stack([rows(a) for a in vsn]),
            jnp.stack(hsn), jnp.stack(gsn))
```

```python
import functools

import numpy as np
import jax
import jax.numpy as jnp
from jax import lax
from jax.experimental import pallas as pl
from jax.experimental.pallas import tpu as pltpu

F32 = jnp.float32
BF16 = jnp.bfloat16

NORM_EPS = 1e-6
HEAD_DIM = 128
HGRN_HEADS = 8
MOBA_HEADS = 8
MOBA_BLOCK = 256
MOBA_TOPK = 3
PAGE_SIZE = 128
GLA_HEADS = 4
GLA_GATE_RANK = 16
GLA_GATE_NORM = 16.0

LANES = 128
MXU_DIM = 256
REC_CHUNK = 128
REC_LEVELS = 7
REC_SMALL_HALVES = (4, 2)
MOBA_KEY_TILE = 4
MOBA_PAGE_GROUP = 8
MASKED = -1e30
VMEM_LIMIT = 56 * 1024 * 1024

NT_DIMS = (((1,), (1,)), ((), ()))
TN_DIMS = (((0,), (0,)), ((), ()))
NEG_INF = float("-inf")


def _params(*sem):
    return pltpu.CompilerParams(dimension_semantics=sem, vmem_limit_bytes=VMEM_LIMIT)


def _rms(x, w):
    return x * lax.rsqrt(jnp.mean(x * x, axis=-1, keepdims=True) + NORM_EPS) * w


def _silu(x):
    return x * jax.nn.sigmoid(x)


def _bdot(a, b):
    return jnp.dot(a.astype(BF16), b.astype(BF16), preferred_element_type=F32)


def _bdot_nt(a, b):
    return lax.dot_general(a.astype(BF16), b.astype(BF16), NT_DIMS, preferred_element_type=F32)


def _ffn_body(x_ref, nw_ref, wg_ref, wu_ref, wd_ref, *rest, final):
    if final:
        fw_ref, o_ref, h_ref, a_ref = rest
    else:
        o_ref, h_ref, a_ref = rest
    j = pl.program_id(1)
    last = pl.num_programs(1) - 1

    def hidden():
        h = h_ref[...]
        g = jnp.dot(h, wg_ref[...], preferred_element_type=F32)
        u = jnp.dot(h, wu_ref[...], preferred_element_type=F32)
        return (_silu(g) * u).astype(BF16)

    def down():
        return jnp.dot(a_ref[...], wd_ref[...], preferred_element_type=F32)

    @pl.when(j == 0)
    def _():
        h_ref[...] = _rms(x_ref[...], nw_ref[...]).astype(BF16)
        o_ref[...] = jnp.zeros_like(o_ref)
        a_ref[...] = hidden()

    @pl.when((j > 0) & (j < last))
    def _():
        d = down()
        a_ref[...] = hidden()
        o_ref[...] += d

    @pl.when(j == last)
    def _():
        y = x_ref[...] + 0.5 * (o_ref[...] + down())
        if final:
            y = _rms(y, fw_ref[...])
        o_ref[...] = y


def ffn(x, nw, wg, wu, wd, layer, which, final_w=None, *, tm, tf=512):
    M, D = x.shape
    FF = wg.shape[-1]
    assert M % tm == 0 and FF % tf == 0
    nf = FF // tf
    final = final_w is not None
    up_tile = lambda i, j: (layer, which, 0, jnp.minimum(j, nf - 1))
    in_specs = [
        pl.BlockSpec((tm, D), lambda i, j: (i, 0)),
        pl.BlockSpec((1, D), lambda i, j: (0, 0)),
        pl.BlockSpec((None, None, D, tf), up_tile),
        pl.BlockSpec((None, None, D, tf), up_tile),
        pl.BlockSpec((None, None, tf, D), lambda i, j: (layer, which, jnp.maximum(j - 1, 0), 0)),
    ]
    args = [x, nw.reshape(1, D), wg, wu, wd]
    if final:
        in_specs.append(pl.BlockSpec((1, D), lambda i, j: (0, 0)))
        args.append(final_w.reshape(1, D))
    return pl.pallas_call(
        functools.partial(_ffn_body, final=final),
        grid=(M // tm, nf + 1),
        in_specs=in_specs,
        out_specs=pl.BlockSpec((tm, D), lambda i, j: (i, 0)),
        out_shape=jax.ShapeDtypeStruct((M, D), F32),
        scratch_shapes=[pltpu.VMEM((tm, D), BF16), pltpu.VMEM((tm, tf), BF16)],
        compiler_params=_params("parallel", "arbitrary"),
        name="ffn",
    )(*args)


def _norm_mm_body(x_ref, nw_ref, w_ref, o_ref, h_ref):
    @pl.when(pl.program_id(1) == 0)
    def _():
        h_ref[...] = _rms(x_ref[...], nw_ref[...]).astype(BF16)

    o_ref[...] = jnp.dot(h_ref[...], w_ref[...], preferred_element_type=F32)


def norm_matmul(x, nw, w, layer, *, tm, tn):
    M, D = x.shape
    N = w.shape[-1]
    assert M % tm == 0 and N % tn == 0
    return pl.pallas_call(
        _norm_mm_body,
        grid=(M // tm, N // tn),
        in_specs=[
            pl.BlockSpec((tm, D), lambda i, j: (i, 0)),
            pl.BlockSpec((1, D), lambda i, j: (0, 0)),
            pl.BlockSpec((None, D, tn), lambda i, j: (layer, 0, j)),
        ],
        out_specs=pl.BlockSpec((tm, tn), lambda i, j: (i, j)),
        out_shape=jax.ShapeDtypeStruct((M, N), F32),
        scratch_shapes=[pltpu.VMEM((tm, D), BF16)],
        compiler_params=_params("parallel", "arbitrary"),
        name="norm_matmul",
    )(x, nw.reshape(1, D), w)


def _out_proj_body(x_ref, a_ref, b_ref, wa_ref, wb_ref, o_ref):
    o_ref[...] = x_ref[...] + _bdot(a_ref[...], wa_ref[...]) + _bdot(b_ref[...], wb_ref[...])


def out_proj(x, a, a_blk, b, b_blk, w, layer, *, tm):
    M, D = x.shape
    K = w.shape[1] // 2
    assert M % tm == 0
    return pl.pallas_call(
        _out_proj_body,
        grid=(M // tm,),
        in_specs=[
            pl.BlockSpec((tm, D), lambda i: (i, 0)),
            pl.BlockSpec((tm, K), lambda i: (i, a_blk)),
            pl.BlockSpec((tm, K), lambda i: (i, b_blk)),
            pl.BlockSpec((None, K, D), lambda i: (layer, 0, 0)),
            pl.BlockSpec((None, K, D), lambda i: (layer, 1, 0)),
        ],
        out_specs=pl.BlockSpec((tm, D), lambda i: (i, 0)),
        out_shape=jax.ShapeDtypeStruct((M, D), F32),
        compiler_params=_params("parallel"),
        name="out_proj",
    )(x, a, b, w, w)


def _rec_constants():
    C = REC_CHUNK
    t = np.arange(C)[:, None]
    s = np.arange(C)[None, :]
    mats = [(s <= t).astype(np.float32)]
    a_mats, b_mats = [], []
    level = np.full((C, C), -1, np.int32)
    level[np.arange(C), np.arange(C)] = REC_LEVELS
    for li in range(REC_LEVELS):
        L = C >> (li + 1)
        mid = (t // (2 * L)) * (2 * L) + L
        if L in REC_SMALL_HALVES:
            a_mats.append(((t >= mid) & (s >= mid) & (s <= t)).astype(np.float32))
            b_mats.append(((t < mid) & (s >= t + 1) & (s <= mid - 1)).astype(np.float32))
        mid_s = (s // (2 * L)) * (2 * L) + L
        level[(t >= mid) & (s < mid_s) & (mid_s == mid)] = li
    w = np.concatenate(mats + a_mats + b_mats, axis=0)
    return jnp.asarray(w, BF16), jnp.asarray(level)


def _level_exponents(li, b, g, e):
    C = REC_CHUNK
    L = C >> (li + 1)
    if L == 1:
        return g, None
    if L in REC_SMALL_HALVES:
        i = REC_SMALL_HALVES.index(L)
        n = len(REC_SMALL_HALVES)
        return e[(1 + i) * C:(2 + i) * C], e[(1 + n + i) * C:(2 + n + i) * C]
    c = jnp.concatenate([jnp.broadcast_to(b[m + L - 1:m + L, :], (2 * L, b.shape[1])) for m in range(0, C, 2 * L)],
                        axis=0)
    d = b - c
    return jnp.minimum(d, 0.0), jnp.minimum(-d, 0.0)


def _rec_core(q, k, vs, g, w_ref, level, st_ref):
    C = REC_CHUNK
    heads = len(vs)
    dk = q.shape[1] // heads
    head = lambda a, h: a[:, h * dk:(h + 1) * dk]
    g1 = g.astype(BF16)
    r1 = g - g1.astype(F32)
    g2 = r1.astype(BF16)
    g3 = (r1 - g2.astype(F32)).astype(BF16)
    w = w_ref[...]
    e = (jnp.dot(w, g1, preferred_element_type=F32) + jnp.dot(w, g2, preferred_element_type=F32)
         + jnp.dot(w, g3, preferred_element_type=F32))
    b = e[0:C]
    b_last = b[C - 1:C, :]
    sts = [st_ref[h] for h in range(heads)]
    qb = (q * jnp.exp(b)).astype(BF16)
    inters = [lax.dot_general(head(qb, h), sts[h].astype(BF16), NT_DIMS, preferred_element_type=F32)
              for h in range(heads)]
    atts = [jnp.where(level == REC_LEVELS, _bdot_nt(head(q, h), head(k, h)), 0.0) for h in range(heads)]
    for li in range(REC_LEVELS):
        eq, ek = _level_exponents(li, b, g, e)
        ql = (q * jnp.exp(eq)).astype(BF16)
        kl = (k if ek is None else k * jnp.exp(ek)).astype(BF16)
        for h in range(heads):
            part = lax.dot_general(head(ql, h), head(kl, h), NT_DIMS, preferred_element_type=F32)
            atts[h] = atts[h] + jnp.where(level == li, part, 0.0)
    kd = (k * jnp.exp(b_last - b)).astype(BF16)
    decay = jnp.exp(b_last)
    outs = []
    for h in range(heads):
        outs.append(inters[h] + _bdot(atts[h], vs[h]))
        upd = lax.dot_general(vs[h].astype(BF16), head(kd, h), TN_DIMS, preferred_element_type=F32)
        st_ref[h] = sts[h] * head(decay, h) + upd
    return outs


def _valid_rows(shape, valid_len):
    return lax.broadcasted_iota(jnp.int32, shape, 0) < valid_len


def _load_state(s0_ref, st_ref):
    @pl.when(pl.program_id(2) == 0)
    def _():
        for hh in range(st_ref.shape[0]):
            st_ref[hh] = s0_ref[hh].T


def _store_state(so_ref, st_ref):
    @pl.when(pl.program_id(2) == pl.num_programs(2) - 1)
    def _():
        for hh in range(st_ref.shape[0]):
            so_ref[hh] = st_ref[hh].T


def _hgrn_body(hq_ref, hf_ref, hi_ref, hg_ref, lb_ref, nw_ref, s0_ref, w_ref, lvl_ref, o_ref, so_ref, st_ref,
               *, valid_len):
    _load_state(s0_ref, st_ref)
    heads, dv, dk = st_ref.shape
    lb = lb_ref[...]
    f = lb + (1.0 - lb) * jax.nn.sigmoid(hf_ref[...])
    k = 1.0 - f
    g = jnp.log(f)
    if valid_len < REC_CHUNK:
        ok = _valid_rows(g.shape, valid_len)
        k = jnp.where(ok, k, 0.0)
        g = jnp.where(ok, g, 0.0)
    q = _silu(hq_ref[...]) * (dk ** -0.5)
    hs = range(heads)
    outs = _rec_core(q, k, [hi_ref[:, h * dv:(h + 1) * dv] for h in hs], g, w_ref, lvl_ref[...], st_ref)
    for h in hs:
        o_ref[:, h * dv:(h + 1) * dv] = _rms(outs[h], nw_ref[...]) * _silu(hg_ref[:, h * dv:(h + 1) * dv])
    _store_state(so_ref, st_ref)


def hgrn_mix(proj, lb, norm_w, s0, *, batch, valid_len=REC_CHUNK, heads_per_step=4):
    H, dk, dv = s0.shape[1:]
    C = REC_CHUNK
    hb = heads_per_step
    assert H % hb == 0
    n = proj.shape[0] // (batch * C)
    wmat, level = _rec_constants()
    row = lambda b, h, c: b * n + c
    col_spec = lambda grp: pl.BlockSpec((C, hb * dk), lambda b, h, c: (row(b, h, c), grp * (H // hb) + h))
    const = lambda shape: pl.BlockSpec(shape, lambda b, h, c: (0,) * len(shape))
    state_spec = pl.BlockSpec((None, hb, dk, dv), lambda b, h, c: (b, h, 0, 0))
    return pl.pallas_call(
        functools.partial(_hgrn_body, valid_len=valid_len),
        grid=(batch, H // hb, n),
        in_specs=[col_spec(0), col_spec(1), col_spec(2), col_spec(3),
                  pl.BlockSpec((1, hb * dk), lambda b, h, c: (0, h)),
                  const((1, dv)), state_spec, const(wmat.shape), const(level.shape)],
        out_specs=[pl.BlockSpec((C, hb * dv), lambda b, h, c: (row(b, h, c), h)), state_spec],
        out_shape=[jax.ShapeDtypeStruct((batch * n * C, H * dv), F32), jax.ShapeDtypeStruct(s0.shape, F32)],
        scratch_shapes=[pltpu.VMEM((hb, dv, dk), F32)],
        compiler_params=_params("parallel", "parallel", "arbitrary"),
        name="hgrn_mix",
    )(proj, proj, proj, proj, lb.reshape(1, H * dk), norm_w.reshape(1, dv), s0, wmat, level)


def _gla_body(q_ref, k_ref, v_ref, r_ref, a_ref, wg_ref, bg_ref, nw_ref, s0_ref, w_ref, lvl_ref, o_ref, so_ref,
              st_ref, *, valid_len):
    _load_state(s0_ref, st_ref)
    heads, dv, dk = st_ref.shape
    x = _bdot(a_ref[...], wg_ref[...]) + bg_ref[...]
    g = -(jnp.maximum(-x, 0.0) + jnp.log(1.0 + jnp.exp(-jnp.abs(x)))) / GLA_GATE_NORM
    k = k_ref[...]
    if valid_len < REC_CHUNK:
        ok = _valid_rows(g.shape, valid_len)
        k = jnp.where(ok, k, 0.0)
        g = jnp.where(ok, g, 0.0)
    q = q_ref[...] * (dk ** -0.5)
    hs = range(heads)
    outs = _rec_core(q, k, [v_ref[:, h * dv:(h + 1) * dv] for h in hs], g, w_ref, lvl_ref[...], st_ref)
    for h in hs:
        o_ref[:, h * dv:(h + 1) * dv] = _rms(outs[h], nw_ref[...]) * _silu(r_ref[:, h * dv:(h + 1) * dv])
    _store_state(so_ref, st_ref)


def gla_mix(proj, w_gate, layer, b_gate, norm_w, s0, *, batch, valid_len=REC_CHUNK, heads_per_step=2):
    H, dk, dv = s0.shape[1:]
    C = REC_CHUNK
    hb = heads_per_step
    assert H % hb == 0
    n = proj.shape[0] // (batch * C)
    nh = H // hb
    wmat, level = _rec_constants()
    row = lambda b, h, c: b * n + c
    const = lambda shape: pl.BlockSpec(shape, lambda b, h, c: (0,) * len(shape))
    state_spec = pl.BlockSpec((None, hb, dk, dv), lambda b, h, c: (b, h, 0, 0))
    kv_off = 2 * H * dk // (hb * dv)
    a_blk = (2 * H * dk + 2 * H * dv) // LANES
    return pl.pallas_call(
        functools.partial(_gla_body, valid_len=valid_len),
        grid=(batch, nh, n),
        in_specs=[pl.BlockSpec((C, hb * dk), lambda b, h, c: (row(b, h, c), h)),
                  pl.BlockSpec((C, hb * dk), lambda b, h, c: (row(b, h, c), nh + h)),
                  pl.BlockSpec((C, hb * dv), lambda b, h, c: (row(b, h, c), kv_off + h)),
                  pl.BlockSpec((C, hb * dv), lambda b, h, c: (row(b, h, c), kv_off + nh + h)),
                  pl.BlockSpec((C, LANES), lambda b, h, c: (row(b, h, c), a_blk)),
                  pl.BlockSpec((None, LANES, hb * dk), lambda b, h, c: (layer, 0, h)),
                  pl.BlockSpec((1, hb * dk), lambda b, h, c: (0, h)),
                  const((1, dv)), state_spec, const(wmat.shape), const(level.shape)],
        out_specs=[pl.BlockSpec((C, hb * dv), lambda b, h, c: (row(b, h, c), h)), state_spec],
        out_shape=[jax.ShapeDtypeStruct((batch * n * C, H * dv), F32), jax.ShapeDtypeStruct(s0.shape, F32)],
        scratch_shapes=[pltpu.VMEM((hb, dv, dk), F32)],
        compiler_params=_params("parallel", "parallel", "arbitrary"),
        name="gla_mix",
    )(proj, proj, proj, proj, proj, w_gate, b_gate.reshape(1, H * dk), norm_w.reshape(1, dv), s0, wmat, level)


def _top_blocks(s, index, axis):
    sel = jnp.zeros(s.shape, F32)
    for _ in range(MOBA_TOPK):
        m = jnp.max(s, axis=axis, keepdims=True)
        idx = jnp.min(jnp.where(s == m, index, s.shape[axis]), axis=axis, keepdims=True)
        hit = index == idx
        sel = jnp.where(hit & (m > NEG_INF), 1.0, sel)
        s = jnp.where(hit, NEG_INF, s)
    return sel


def _moba_prompt_body(q_ref, k_ref, v_ref, o_ref, kmean_ref, kaug_ref, vt_ref, lg_ref):
    i = pl.program_id(2)
    blk = MOBA_BLOCK
    tile = MOBA_KEY_TILE * blk
    T = k_ref.shape[0]
    nb = T // blk
    nbp = kmean_ref.shape[0]
    dh = q_ref.shape[-1]

    @pl.when(i == 0)
    def _():
        kmean_ref[...] = jnp.zeros_like(kmean_ref)
        for jb in range(nb):
            rows = slice(jb * blk, (jb + 1) * blk)
            kmean_ref[jb:jb + 1, :] = jnp.mean(k_ref[rows, :], axis=0, keepdims=True)
            vt_ref[:, rows] = v_ref[rows, :].T.astype(BF16)
        kaug_ref[:, 0:dh] = k_ref[...].astype(BF16)
        key_blk = lax.broadcasted_iota(jnp.int32, (T, LANES), 0) // blk
        on_blk = lax.broadcasted_iota(jnp.int32, (T, LANES), 1) == key_blk
        kaug_ref[:, dh:dh + LANES] = jnp.where(on_blk, MASKED, 0.0).astype(BF16)

    q = q_ref[...]
    s = lax.dot_general(kmean_ref[...], q, NT_DIMS, precision=lax.Precision.HIGHEST, preferred_element_type=F32)
    blk_id = lax.broadcasted_iota(jnp.int32, (nbp, blk), 0)
    sel = _top_blocks(jnp.where(blk_id < i, s, NEG_INF), blk_id, 0)
    qst = (q * (dh ** -0.5)).T.astype(BF16)
    unpicked = jnp.concatenate([1.0 - sel, jnp.zeros((LANES - nbp, blk), F32)], axis=0).astype(BF16)
    qaug = jnp.concatenate([qst, unpicked], axis=0)

    own = pl.multiple_of(i * blk, blk)
    lg_own = jnp.dot(kaug_ref[pl.ds(own, blk), 0:dh], qst, preferred_element_type=F32)
    visible = lax.broadcasted_iota(jnp.int32, (blk, blk), 0) <= lax.broadcasted_iota(jnp.int32, (blk, blk), 1)
    lg_own = jnp.where(visible, lg_own, NEG_INF)
    lg_ref[T:T + blk, :] = lg_own
    n_tiles = (i + MOBA_KEY_TILE - 1) // MOBA_KEY_TILE

    def score(t, m):
        start = pl.multiple_of(t * tile, tile)
        lg = jnp.dot(kaug_ref[pl.ds(start, tile), :], qaug, preferred_element_type=F32)
        lg_ref[pl.ds(start, tile), :] = lg
        return jnp.maximum(m, jnp.max(lg, axis=0, keepdims=True))

    m = lax.fori_loop(0, n_tiles, score, jnp.max(lg_own, axis=0, keepdims=True))

    p_own = jnp.exp(lg_ref[T:T + blk, :] - m)
    acc0 = jnp.dot(vt_ref[:, pl.ds(own, blk)], p_own.astype(BF16), preferred_element_type=F32)

    def gather(t, carry):
        l, acc = carry
        start = pl.multiple_of(t * tile, tile)
        p = jnp.exp(lg_ref[pl.ds(start, tile), :] - m)
        acc = acc + jnp.dot(vt_ref[:, pl.ds(start, tile)], p.astype(BF16), preferred_element_type=F32)
        return l + jnp.sum(p, axis=0, keepdims=True), acc

    l, acc = lax.fori_loop(0, n_tiles, gather, (jnp.sum(p_own, axis=0, keepdims=True), acc0))
    o_ref[...] = (acc / l).T


def moba_prompt(proj, *, batch, q_col, k_col, v_col):
    H, dh, blk = MOBA_HEADS, HEAD_DIM, MOBA_BLOCK
    T = proj.shape[0] // batch
    assert T % (MOBA_KEY_TILE * blk) == 0 and T // blk <= LANES
    nq = T // blk
    nbp = -(-nq // 8) * 8
    return pl.pallas_call(
        _moba_prompt_body,
        grid=(batch, H, nq),
        in_specs=[pl.BlockSpec((blk, dh), lambda b, h, i: (b * nq + i, q_col + h)),
                  pl.BlockSpec((T, dh), lambda b, h, i: (b, k_col + h)),
                  pl.BlockSpec((T, dh), lambda b, h, i: (b, v_col + h))],
        out_specs=pl.BlockSpec((blk, dh), lambda b, h, i: (b * nq + i, h)),
        out_shape=jax.ShapeDtypeStruct((batch * T, H * dh), F32),
        scratch_shapes=[pltpu.VMEM((nbp, dh), F32), pltpu.VMEM((T, dh + LANES), BF16),
                        pltpu.VMEM((dh, T), BF16), pltpu.VMEM((T + blk, blk), F32)],
        compiler_params=_params("parallel", "parallel", "arbitrary"),
        name="moba_prompt",
    )(proj, proj, proj)


def _stack_heads(q):
    heads = q.shape[1] // HEAD_DIM
    return jnp.concatenate([q[:, h * HEAD_DIM:(h + 1) * HEAD_DIM] for h in range(heads)], axis=0)


def _same_head(rows, cols, lq, heads):
    r = lax.broadcasted_iota(jnp.int32, (rows, cols), 0)
    c = lax.broadcasted_iota(jnp.int32, (rows, cols), 1)
    return (r // lq) == (c % heads)


def _moba_pages_body(pt_ref, q_ref, *refs):
    group = MOBA_PAGE_GROUP
    k_refs, v_refs = refs[:group], refs[group:2 * group]
    sc_ref, m_ref, l_ref, o_ref = refs[2 * group:]
    step = pl.program_id(1)
    lq = q_ref.shape[0]
    n_tok, heads, dh = k_refs[0].shape

    @pl.when(step == 0)
    def _():
        sc_ref[...] = jnp.zeros_like(sc_ref)
        m_ref[...] = jnp.zeros_like(m_ref)
        l_ref[...] = jnp.zeros_like(l_ref)

    qa = _stack_heads(q_ref[...])
    rows = qa.shape[0]
    qs = (qa * (dh ** -0.5)).astype(BF16)
    same = _same_head(rows, n_tok * heads, lq, heads)
    lane = lax.broadcasted_iota(jnp.int32, sc_ref.shape, 1)
    sc_all, m_all, l_all = sc_ref[...], m_ref[...], l_ref[...]
    for gi in range(group):
        kp = k_refs[gi][...]
        ksum = jnp.sum(kp, axis=0)
        ksum_rows = jnp.concatenate([jnp.broadcast_to(ksum[h:h + 1, :], (lq, dh)) for h in range(heads)], axis=0)
        sc = jnp.sum(qa * ksum_rows, axis=1, keepdims=True)
        logits = lax.dot_general(qs, kp.reshape(n_tok * heads, dh).astype(BF16), NT_DIMS,
                                 preferred_element_type=F32)
        logits = jnp.where(same, logits, NEG_INF)
        m = jnp.max(logits, axis=1, keepdims=True)
        pr = jnp.exp(logits - m)
        l = jnp.sum(pr, axis=1, keepdims=True)
        o_ref[gi] = _bdot(pr, v_refs[gi][...].reshape(n_tok * heads, dh))
        here = lane == step * group + gi
        sc_all = jnp.where(here, sc, sc_all)
        m_all = jnp.where(here, m, m_all)
        l_all = jnp.where(here, l, l_all)
    sc_ref[...] = sc_all
    m_ref[...] = m_all
    l_ref[...] = l_all


def moba_pages(q, cache_k, cache_v, page_table, layer):
    B, lq, width = q.shape
    _, _, n_tok, heads, dh = cache_k.shape
    n_pages = page_table.shape[1]
    group = MOBA_PAGE_GROUP
    assert n_pages == LANES and heads * dh == width and n_pages % group == 0
    rows = heads * lq
    page_spec = lambda gi: pl.BlockSpec((None, None, n_tok, heads, dh),
                                        lambda b, p, pt: (layer, pt[b, p * group + gi], 0, 0, 0))
    stat_spec = pl.BlockSpec((None, rows, LANES), lambda b, p, pt: (b, 0, 0))
    stat = jax.ShapeDtypeStruct((B, rows, LANES), F32)
    page_specs = [page_spec(gi) for gi in range(group)]
    return pl.pallas_call(
        _moba_pages_body,
        grid_spec=pltpu.PrefetchScalarGridSpec(
            num_scalar_prefetch=1,
            grid=(B, n_pages // group),
            in_specs=[pl.BlockSpec((None, lq, width), lambda b, p, pt: (b, 0, 0))] + page_specs + page_specs,
            out_specs=[stat_spec, stat_spec, stat_spec,
                       pl.BlockSpec((None, group, rows, dh), lambda b, p, pt: (b, p, 0, 0))],
        ),
        out_shape=[stat, stat, stat, jax.ShapeDtypeStruct((B, n_pages, rows, dh), F32)],
        compiler_params=_params("parallel", "arbitrary"),
        name="moba_pages",
    )(page_table, q, *([cache_k] * group), *([cache_v] * group))


def _moba_combine_body(sc_ref, m_ref, l_ref, op_ref, q_ref, kn_ref, vn_ref, o_ref):
    lq = q_ref.shape[0]
    rows = sc_ref.shape[0]
    heads = rows // lq
    n_pages = op_ref.shape[0]
    pages_per_block = MOBA_BLOCK // PAGE_SIZE
    assert pages_per_block == 2
    lane = lax.broadcasted_iota(jnp.int32, (rows, LANES), 1)
    sc = sc_ref[...]
    bs = (sc + pltpu.roll(sc, LANES - 1, 1)) * (1.0 / MOBA_BLOCK)
    sel = _top_blocks(jnp.where(lane % pages_per_block == 0, bs, NEG_INF), lane, 1)
    sel = sel + pltpu.roll(sel, 1, 1)

    qa = _stack_heads(q_ref[...])
    lo = _bdot_nt(qa * (HEAD_DIM ** -0.5), kn_ref[...])
    t_q = lax.broadcasted_iota(jnp.int32, (rows, LANES), 0) % lq
    lo = jnp.where(_same_head(rows, LANES, lq, heads) & (lane // heads <= t_q), lo, NEG_INF)
    m = m_ref[...]
    m_all = jnp.maximum(jnp.max(lo, axis=1, keepdims=True),
                        jnp.max(jnp.where(sel > 0.0, m, NEG_INF), axis=1, keepdims=True))
    w = jnp.where(sel > 0.0, jnp.exp(m - m_all), 0.0)
    p_own = jnp.exp(lo - m_all)
    den = jnp.sum(w * l_ref[...], axis=1, keepdims=True) + jnp.sum(p_own, axis=1, keepdims=True)
    num0 = _bdot(p_own, vn_ref[...])

    def body(p, num):
        wp = jnp.sum(jnp.where(lane == p, w, 0.0), axis=1, keepdims=True)
        return num + wp * op_ref[p]

    out = lax.fori_loop(0, n_pages, body, num0) / den
    o_ref[...] = jnp.concatenate([out[h * lq:(h + 1) * lq, :] for h in range(heads)], axis=1)


def moba_combine(sc, m, l, o_pages, q, k_new, v_new):
    B, lq, width = q.shape
    heads, dh = k_new.shape[2:]
    rows = sc.shape[1]
    n_pages = o_pages.shape[1]
    assert LANES % heads == 0 and lq * heads <= LANES
    pad = ((0, 0), (0, LANES // heads - lq), (0, 0), (0, 0))
    kn = jnp.pad(k_new, pad).reshape(B, LANES, dh)
    vn = jnp.pad(v_new, pad).reshape(B, LANES, dh)
    stat_spec = pl.BlockSpec((None, rows, LANES), lambda b: (b, 0, 0))
    new_spec = pl.BlockSpec((None, LANES, dh), lambda b: (b, 0, 0))
    q_spec = pl.BlockSpec((None, lq, width), lambda b: (b, 0, 0))
    return pl.pallas_call(
        _moba_combine_body,
        grid=(B,),
        in_specs=[stat_spec, stat_spec, stat_spec,
                  pl.BlockSpec((None, n_pages, rows, dh), lambda b: (b, 0, 0, 0)),
                  q_spec, new_spec, new_spec],
        out_specs=q_spec,
        out_shape=jax.ShapeDtypeStruct((B, lq, width), F32),
        compiler_params=_params("parallel"),
        name="moba_combine",
    )(sc, m, l, o_pages, q, kn, vn)


def _pad_tokens(a, batch, length):
    a = a.reshape(batch, -1, a.shape[-1])
    return jnp.pad(a, ((0, 0), (0, length - a.shape[1]), (0, 0))).reshape(batch * length, a.shape[-1])


def _unpad_tokens(a, batch, seq):
    return a.reshape(batch, -1, a.shape[-1])[:, :seq].reshape(batch * seq, a.shape[-1])


def _tiles(rows):
    if rows >= 1024:
        return 512, 1024, 1024, 1280
    return rows, rows, 1024, 1280


def _trunk(x, batch, hgrn_s0, gla_s0, past, lb, wts):
    (norm_w, ffn_gate, ffn_up, ffn_down, even_w_in, even_w_out, hgrn_norm_w, gla_w_in, gla_w_gate,
     gla_b_gate, gla_norm_w, gla_w_out, final_norm_w) = wts
    depth = norm_w.shape[0]
    seq = x.shape[0] // batch
    tm_ffn, tm, tn_even, tn_odd = _tiles(x.shape[0])
    padded = seq % REC_CHUNK != 0
    hw = HGRN_HEADS * HEAD_DIM
    mw = MOBA_HEADS * HEAD_DIM
    ks, vs, hs, gs = [], [], [], []
    for l in range(depth):
        x = ffn(x, norm_w[l, 0], ffn_gate, ffn_up, ffn_down, l, 0, tm=tm_ffn)
        if l % 2 == 0:
            e = l // 2
            proj = norm_matmul(x, norm_w[l, 1], even_w_in, e, tm=tm, tn=tn_even)
            rec_in = _pad_tokens(proj[:, :4 * hw], batch, REC_CHUNK) if padded else proj
            o_h, s_h = hgrn_mix(rec_in, lb[e], hgrn_norm_w[e], hgrn_s0[e], batch=batch,
                                valid_len=seq if padded else REC_CHUNK)
            if padded:
                o_h = _unpad_tokens(o_h, batch, seq)
            k_new = proj[:, 4 * hw + mw:4 * hw + 2 * mw]
            v_new = proj[:, 4 * hw + 2 * mw:]
            if past is None:
                first = 4 * hw // HEAD_DIM
                o_m = moba_prompt(proj, batch=batch, q_col=first, k_col=first + MOBA_HEADS,
                                  v_col=first + 2 * MOBA_HEADS)
            else:
                cache_k, cache_v, page_table = past
                q = proj[:, 4 * hw:4 * hw + mw].reshape(batch, seq, mw)
                sc, m, lsum, o_pages = moba_pages(q, cache_k, cache_v, page_table, e)
                heads = lambda a: a.reshape(batch, seq, MOBA_HEADS, HEAD_DIM)
                o_m = moba_combine(sc, m, lsum, o_pages, q, heads(k_new), heads(v_new)).reshape(batch * seq, mw)
            x = out_proj(x, o_h, 0, o_m, 0, even_w_out, e, tm=tm_ffn)
            ks.append(k_new)
            vs.append(v_new)
            hs.append(s_h)
        else:
            o = l // 2
            proj = norm_matmul(x, norm_w[l, 1], gla_w_in, o, tm=tm, tn=tn_odd)
            rec_in = _pad_tokens(proj, batch, REC_CHUNK) if padded else proj
            o_g, s_g = gla_mix(rec_in, gla_w_gate, o, gla_b_gate[o], gla_norm_w[o], gla_s0[o], batch=batch,
                               valid_len=seq if padded else REC_CHUNK)
            if padded:
                o_g = _unpad_tokens(o_g, batch, seq)
            x = out_proj(x, o_g, 0, o_g, 1, gla_w_out, o, tm=tm_ffn)
            gs.append(s_g)
        x = ffn(x, norm_w[l, 2], ffn_gate, ffn_up, ffn_down, l, 1,
                final_norm_w if l == depth - 1 else None, tm=tm_ffn)
    return x, ks, vs, hs, gs


def kernel(x_prompt, x_sample, cache_k, cache_v, state_hgrn, state_gla, page_table, norm_w, ffn_gate, ffn_up,
           ffn_down, even_w_in, even_w_out, hgrn_lb_logits, hgrn_norm_w, gla_w_in, gla_w_gate_up, gla_b_gate,
           gla_norm_w, gla_w_out, final_norm_w):
    lb = jnp.cumsum(jax.nn.softmax(hgrn_lb_logits.astype(F32), axis=0), axis=0)
    lb = lb - lb[0:1]

    n_odd, d_model, odd_in = gla_w_in.shape
    gate_col = odd_in - GLA_GATE_RANK
    assert gate_col % LANES == 0
    odd_cols = -(-(gate_col + LANES) // (5 * MXU_DIM)) * (5 * MXU_DIM)
    gla_w_in_p = jnp.pad(gla_w_in, ((0, 0), (0, 0), (0, odd_cols - odd_in))).astype(BF16)
    gla_w_gate_p = jnp.pad(gla_w_gate_up, ((0, 0), (0, LANES - GLA_GATE_RANK), (0, 0))).astype(BF16)
    wts = (norm_w, ffn_gate.astype(BF16), ffn_up.astype(BF16), ffn_down.astype(BF16), even_w_in.astype(BF16),
           even_w_out.astype(BF16), hgrn_norm_w, gla_w_in_p, gla_w_gate_p, gla_b_gate, gla_norm_w,
           gla_w_out.astype(BF16), final_norm_w)

    Bp, Lp, D = x_prompt.shape
    n_even = state_hgrn.shape[0]
    h0 = jnp.zeros((n_even, Bp) + state_hgrn.shape[2:], F32)
    g0 = jnp.zeros((n_odd, Bp) + state_gla.shape[2:], F32)
    y_p, kp, vp, hp, gp = _trunk(x_prompt.reshape(Bp * Lp, D), Bp, h0, g0, None, lb, wts)

    Bs, Ls, _ = x_sample.shape
    y_s, ksn, vsn, hsn, gsn = _trunk(x_sample.reshape(Bs * Ls, D), Bs, state_hgrn, state_gla,
                                     (cache_k, cache_v, page_table), lb, wts)

    pages = lambda a: a.reshape(Bp, Lp // PAGE_SIZE, PAGE_SIZE, MOBA_HEADS, HEAD_DIM)
    rows = lambda a: a.reshape(Bs, Ls, MOBA_HEADS, HEAD_DIM)
    return (y_p.reshape(Bp, Lp, D), y_s.reshape(Bs, Ls, D),
            jnp.stack([pages(a) for a in kp]), jnp.stack([pages(a) for a in vp]),
            jnp.stack(hp), jnp.stack(gp),
            jnp.stack([rows(a) for a in ksn]), jnp.stack([rows(a) for a in vsn]),
            jnp.stack(hsn), jnp.stack(gsn))
```

```python
import functools

import numpy as np
import jax
import jax.numpy as jnp
from jax import lax
from jax.experimental import pallas as pl
from jax.experimental.pallas import tpu as pltpu

F32 = jnp.float32
BF16 = jnp.bfloat16

NORM_EPS = 1e-6
HEAD_DIM = 128
HGRN_HEADS = 8
MOBA_HEADS = 8
MOBA_BLOCK = 256
MOBA_TOPK = 3
PAGE_SIZE = 128
GLA_HEADS = 4
GLA_GATE_RANK = 16
GLA_GATE_NORM = 16.0

LANES = 128
MXU_DIM = 256
REC_CHUNK = 128
REC_LEVELS = 7
REC_SMALL_HALVES = (4, 2)
MOBA_KEY_TILE = 4
MOBA_PAGE_GROUP = 8
MASKED = -1e30
VMEM_LIMIT = 60 * 1024 * 1024

NT_DIMS = (((1,), (1,)), ((), ()))
TN_DIMS = (((0,), (0,)), ((), ()))
NEG_INF = float("-inf")


def _params(*sem):
    return pltpu.CompilerParams(dimension_semantics=sem, vmem_limit_bytes=VMEM_LIMIT)


def _rms(x, w):
    return x * lax.rsqrt(jnp.mean(x * x, axis=-1, keepdims=True) + NORM_EPS) * w


def _silu(x):
    return x * jax.nn.sigmoid(x)


def _bdot(a, b):
    return jnp.dot(a.astype(BF16), b.astype(BF16), preferred_element_type=F32)


def _bdot_nt(a, b):
    return lax.dot_general(a.astype(BF16), b.astype(BF16), NT_DIMS, preferred_element_type=F32)


def _ffn_body(x_ref, nw_ref, wg_ref, wu_ref, wd_ref, *rest, final):
    if final:
        fw_ref, o_ref, h_ref, a_ref = rest
    else:
        o_ref, h_ref, a_ref = rest
    j = pl.program_id(1)
    last = pl.num_programs(1) - 1

    def hidden():
        h = h_ref[...]
        g = jnp.dot(h, wg_ref[...], preferred_element_type=F32)
        u = jnp.dot(h, wu_ref[...], preferred_element_type=F32)
        return (_silu(g) * u).astype(BF16)

    def down():
        return jnp.dot(a_ref[...], wd_ref[...], preferred_element_type=F32)

    @pl.when(j == 0)
    def _():
        h_ref[...] = _rms(x_ref[...], nw_ref[...]).astype(BF16)
        o_ref[...] = jnp.zeros_like(o_ref)
        a_ref[...] = hidden()

    @pl.when((j > 0) & (j < last))
    def _():
        d = down()
        a_ref[...] = hidden()
        o_ref[...] += d

    @pl.when(j == last)
    def _():
        y = x_ref[...] + 0.5 * (o_ref[...] + down())
        if final:
            y = _rms(y, fw_ref[...])
        o_ref[...] = y


def ffn(x, nw, wg, wu, wd, layer, which, final_w=None, *, tm, tf=512):
    M, D = x.shape
    FF = wg.shape[-1]
    assert M % tm == 0 and FF % tf == 0
    nf = FF // tf
    final = final_w is not None
    up_tile = lambda i, j: (layer, which, 0, jnp.minimum(j, nf - 1))
    in_specs = [
        pl.BlockSpec((tm, D), lambda i, j: (i, 0)),
        pl.BlockSpec((1, D), lambda i, j: (0, 0)),
        pl.BlockSpec((None, None, D, tf), up_tile),
        pl.BlockSpec((None, None, D, tf), up_tile),
        pl.BlockSpec((None, None, tf, D), lambda i, j: (layer, which, jnp.maximum(j - 1, 0), 0)),
    ]
    args = [x, nw.reshape(1, D), wg, wu, wd]
    if final:
        in_specs.append(pl.BlockSpec((1, D), lambda i, j: (0, 0)))
        args.append(final_w.reshape(1, D))
    return pl.pallas_call(
        functools.partial(_ffn_body, final=final),
        grid=(M // tm, nf + 1),
        in_specs=in_specs,
        out_specs=pl.BlockSpec((tm, D), lambda i, j: (i, 0)),
        out_shape=jax.ShapeDtypeStruct((M, D), F32),
        scratch_shapes=[pltpu.VMEM((tm, D), BF16), pltpu.VMEM((tm, tf), BF16)],
        compiler_params=_params("parallel", "arbitrary"),
        name="ffn",
    )(*args)


def _norm_mm_body(x_ref, nw_ref, w_ref, o_ref, h_ref):
    @pl.when(pl.program_id(1) == 0)
    def _():
        h_ref[...] = _rms(x_ref[...], nw_ref[...]).astype(BF16)

    o_ref[...] = jnp.dot(h_ref[...], w_ref[...], preferred_element_type=F32)


def norm_matmul(x, nw, w, layer, *, tm, tn):
    M, D = x.shape
    N = w.shape[-1]
    assert M % tm == 0 and N % tn == 0
    return pl.pallas_call(
        _norm_mm_body,
        grid=(M // tm, N // tn),
        in_specs=[
            pl.BlockSpec((tm, D), lambda i, j: (i, 0)),
            pl.BlockSpec((1, D), lambda i, j: (0, 0)),
            pl.BlockSpec((None, D, tn), lambda i, j: (layer, 0, j)),
        ],
        out_specs=pl.BlockSpec((tm, tn), lambda i, j: (i, j)),
        out_shape=jax.ShapeDtypeStruct((M, N), F32),
        scratch_shapes=[pltpu.VMEM((tm, D), BF16)],
        compiler_params=_params("parallel", "arbitrary"),
        name="norm_matmul",
    )(x, nw.reshape(1, D), w)


def _out_proj_body(x_ref, a_ref, b_ref, wa_ref, wb_ref, o_ref):
    o_ref[...] = x_ref[...] + _bdot(a_ref[...], wa_ref[...]) + _bdot(b_ref[...], wb_ref[...])


def out_proj(x, a, a_blk, b, b_blk, w, layer, *, tm):
    M, D = x.shape
    K = w.shape[1] // 2
    assert M % tm == 0
    return pl.pallas_call(
        _out_proj_body,
        grid=(M // tm,),
        in_specs=[
            pl.BlockSpec((tm, D), lambda i: (i, 0)),
            pl.BlockSpec((tm, K), lambda i: (i, a_blk)),
            pl.BlockSpec((tm, K), lambda i: (i, b_blk)),
            pl.BlockSpec((None, K, D), lambda i: (layer, 0, 0)),
            pl.BlockSpec((None, K, D), lambda i: (layer, 1, 0)),
        ],
        out_specs=pl.BlockSpec((tm, D), lambda i: (i, 0)),
        out_shape=jax.ShapeDtypeStruct((M, D), F32),
        compiler_params=_params("parallel"),
        name="out_proj",
    )(x, a, b, w, w)


def _rec_constants():
    C = REC_CHUNK
    t = np.arange(C)[:, None]
    s = np.arange(C)[None, :]
    mats = [(s <= t).astype(np.float32)]
    a_mats, b_mats = [], []
    level = np.full((C, C), -1, np.int32)
    level[np.arange(C), np.arange(C)] = REC_LEVELS
    for li in range(REC_LEVELS):
        L = C >> (li + 1)
        mid = (t // (2 * L)) * (2 * L) + L
        if L in REC_SMALL_HALVES:
            a_mats.append(((t >= mid) & (s >= mid) & (s <= t)).astype(np.float32))
            b_mats.append(((t < mid) & (s >= t + 1) & (s <= mid - 1)).astype(np.float32))
        mid_s = (s // (2 * L)) * (2 * L) + L
        level[(t >= mid) & (s < mid_s) & (mid_s == mid)] = li
    w = np.concatenate(mats + a_mats + b_mats, axis=0)
    return jnp.asarray(w, BF16), jnp.asarray(level)


def _level_exponents(li, b, g, e):
    C = REC_CHUNK
    L = C >> (li + 1)
    if L == 1:
        return g, None
    if L in REC_SMALL_HALVES:
        i = REC_SMALL_HALVES.index(L)
        n = len(REC_SMALL_HALVES)
        return e[(1 + i) * C:(2 + i) * C], e[(1 + n + i) * C:(2 + n + i) * C]
    c = jnp.concatenate([jnp.broadcast_to(b[m + L - 1:m + L, :], (2 * L, b.shape[1])) for m in range(0, C, 2 * L)],
                        axis=0)
    d = b - c
    return jnp.minimum(d, 0.0), jnp.minimum(-d, 0.0)


def _rec_core(q, k, vs, g, w_ref, level, st_ref):
    C = REC_CHUNK
    heads = len(vs)
    dk = q.shape[1] // heads
    head = lambda a, h: a[:, h * dk:(h + 1) * dk]
    g1 = g.astype(BF16)
    r1 = g - g1.astype(F32)
    g2 = r1.astype(BF16)
    g3 = (r1 - g2.astype(F32)).astype(BF16)
    w = w_ref[...]
    e = (jnp.dot(w, g1, preferred_element_type=F32) + jnp.dot(w, g2, preferred_element_type=F32)
         + jnp.dot(w, g3, preferred_element_type=F32))
    b = e[0:C]
    b_last = b[C - 1:C, :]
    sts = [st_ref[h] for h in range(heads)]
    qb = (q * jnp.exp(b)).astype(BF16)
    inters = [lax.dot_general(head(qb, h), sts[h].astype(BF16), NT_DIMS, preferred_element_type=F32)
              for h in range(heads)]
    atts = [jnp.where(level == REC_LEVELS, _bdot_nt(head(q, h), head(k, h)), 0.0) for h in range(heads)]
    for li in range(REC_LEVELS):
        eq, ek = _level_exponents(li, b, g, e)
        ql = (q * jnp.exp(eq)).astype(BF16)
        kl = (k if ek is None else k * jnp.exp(ek)).astype(BF16)
        for h in range(heads):
            part = lax.dot_general(head(ql, h), head(kl, h), NT_DIMS, preferred_element_type=F32)
            atts[h] = atts[h] + jnp.where(level == li, part, 0.0)
    kd = (k * jnp.exp(b_last - b)).astype(BF16)
    decay = jnp.exp(b_last)
    outs = []
    for h in range(heads):
        outs.append(inters[h] + _bdot(atts[h], vs[h]))
        upd = lax.dot_general(vs[h].astype(BF16), head(kd, h), TN_DIMS, preferred_element_type=F32)
        st_ref[h] = sts[h] * head(decay, h) + upd
    return outs


def _valid_rows(shape, valid_len):
    return lax.broadcasted_iota(jnp.int32, shape, 0) < valid_len


def _load_state(s0_ref, st_ref):
    @pl.when(pl.program_id(2) == 0)
    def _():
        for hh in range(st_ref.shape[0]):
            st_ref[hh] = s0_ref[hh].T


def _store_state(so_ref, st_ref):
    @pl.when(pl.program_id(2) == pl.num_programs(2) - 1)
    def _():
        for hh in range(st_ref.shape[0]):
            so_ref[hh] = st_ref[hh].T


def _hgrn_body(hq_ref, hf_ref, hi_ref, hg_ref, lb_ref, nw_ref, s0_ref, w_ref, lvl_ref, o_ref, so_ref, st_ref,
               *, valid_len):
    _load_state(s0_ref, st_ref)
    heads, dv, dk = st_ref.shape
    lb = lb_ref[...]
    f = lb + (1.0 - lb) * jax.nn.sigmoid(hf_ref[...])
    k = 1.0 - f
    g = jnp.log(f)
    if valid_len < REC_CHUNK:
        ok = _valid_rows(g.shape, valid_len)
        k = jnp.where(ok, k, 0.0)
        g = jnp.where(ok, g, 0.0)
    q = _silu(hq_ref[...]) * (dk ** -0.5)
    hs = range(heads)
    outs = _rec_core(q, k, [hi_ref[:, h * dv:(h + 1) * dv] for h in hs], g, w_ref, lvl_ref[...], st_ref)
    for h in hs:
        o_ref[:, h * dv:(h + 1) * dv] = _rms(outs[h], nw_ref[...]) * _silu(hg_ref[:, h * dv:(h + 1) * dv])
    _store_state(so_ref, st_ref)


def hgrn_mix(proj, lb, norm_w, s0, *, batch, valid_len=REC_CHUNK, heads_per_step=8):
    H, dk, dv = s0.shape[1:]
    C = REC_CHUNK
    hb = heads_per_step
    assert H % hb == 0
    n = proj.shape[0] // (batch * C)
    wmat, level = _rec_constants()
    row = lambda b, h, c: b * n + c
    col_spec = lambda grp: pl.BlockSpec((C, hb * dk), lambda b, h, c: (row(b, h, c), grp * (H // hb) + h))
    const = lambda shape: pl.BlockSpec(shape, lambda b, h, c: (0,) * len(shape))
    state_spec = pl.BlockSpec((None, hb, dk, dv), lambda b, h, c: (b, h, 0, 0))
    return pl.pallas_call(
        functools.partial(_hgrn_body, valid_len=valid_len),
        grid=(batch, H // hb, n),
        in_specs=[col_spec(0), col_spec(1), col_spec(2), col_spec(3),
                  pl.BlockSpec((1, hb * dk), lambda b, h, c: (0, h)),
                  const((1, dv)), state_spec, const(wmat.shape), const(level.shape)],
        out_specs=[pl.BlockSpec((C, hb * dv), lambda b, h, c: (row(b, h, c), h)), state_spec],
        out_shape=[jax.ShapeDtypeStruct((batch * n * C, H * dv), F32), jax.ShapeDtypeStruct(s0.shape, F32)],
        scratch_shapes=[pltpu.VMEM((hb, dv, dk), F32)],
        compiler_params=_params("parallel", "parallel", "arbitrary"),
        name="hgrn_mix",
    )(proj, proj, proj, proj, lb.reshape(1, H * dk), norm_w.reshape(1, dv), s0, wmat, level)


def _gla_body(q_ref, k_ref, v_ref, r_ref, a_ref, wg_ref, bg_ref, nw_ref, s0_ref, w_ref, lvl_ref, o_ref, so_ref,
              st_ref, *, valid_len):
    _load_state(s0_ref, st_ref)
    heads, dv, dk = st_ref.shape
    x = _bdot(a_ref[...], wg_ref[...]) + bg_ref[...]
    g = -(jnp.maximum(-x, 0.0) + jnp.log(1.0 + jnp.exp(-jnp.abs(x)))) / GLA_GATE_NORM
    k = k_ref[...]
    if valid_len < REC_CHUNK:
        ok = _valid_rows(g.shape, valid_len)
        k = jnp.where(ok, k, 0.0)
        g = jnp.where(ok, g, 0.0)
    q = q_ref[...] * (dk ** -0.5)
    hs = range(heads)
    outs = _rec_core(q, k, [v_ref[:, h * dv:(h + 1) * dv] for h in hs], g, w_ref, lvl_ref[...], st_ref)
    for h in hs:
        o_ref[:, h * dv:(h + 1) * dv] = _rms(outs[h], nw_ref[...]) * _silu(r_ref[:, h * dv:(h + 1) * dv])
    _store_state(so_ref, st_ref)


def gla_mix(proj, w_gate, layer, b_gate, norm_w, s0, *, batch, valid_len=REC_CHUNK, heads_per_step=4):
    H, dk, dv = s0.shape[1:]
    C = REC_CHUNK
    hb = heads_per_step
    assert H % hb == 0
    n = proj.shape[0] // (batch * C)
    nh = H // hb
    wmat, level = _rec_constants()
    row = lambda b, h, c: b * n + c
    const = lambda shape: pl.BlockSpec(shape, lambda b, h, c: (0,) * len(shape))
    state_spec = pl.BlockSpec((None, hb, dk, dv), lambda b, h, c: (b, h, 0, 0))
    kv_off = 2 * H * dk // (hb * dv)
    a_blk = (2 * H * dk + 2 * H * dv) // LANES
    return pl.pallas_call(
        functools.partial(_gla_body, valid_len=valid_len),
        grid=(batch, nh, n),
        in_specs=[pl.BlockSpec((C, hb * dk), lambda b, h, c: (row(b, h, c), h)),
                  pl.BlockSpec((C, hb * dk), lambda b, h, c: (row(b, h, c), nh + h)),
                  pl.BlockSpec((C, hb * dv), lambda b, h, c: (row(b, h, c), kv_off + h)),
                  pl.BlockSpec((C, hb * dv), lambda b, h, c: (row(b, h, c), kv_off + nh + h)),
                  pl.BlockSpec((C, LANES), lambda b, h, c: (row(b, h, c), a_blk)),
                  pl.BlockSpec((None, LANES, hb * dk), lambda b, h, c: (layer, 0, h)),
                  pl.BlockSpec((1, hb * dk), lambda b, h, c: (0, h)),
                  const((1, dv)), state_spec, const(wmat.shape), const(level.shape)],
        out_specs=[pl.BlockSpec((C, hb * dv), lambda b, h, c: (row(b, h, c), h)), state_spec],
        out_shape=[jax.ShapeDtypeStruct((batch * n * C, H * dv), F32), jax.ShapeDtypeStruct(s0.shape, F32)],
        scratch_shapes=[pltpu.VMEM((hb, dv, dk), F32)],
        compiler_params=_params("parallel", "parallel", "arbitrary"),
        name="gla_mix",
    )(proj, proj, proj, proj, proj, w_gate, b_gate.reshape(1, H * dk), norm_w.reshape(1, dv), s0, wmat, level)


def _top_blocks(s, index, axis):
    sel = jnp.zeros(s.shape, F32)
    for _ in range(MOBA_TOPK):
        m = jnp.max(s, axis=axis, keepdims=True)
        idx = jnp.min(jnp.where(s == m, index, s.shape[axis]), axis=axis, keepdims=True)
        hit = index == idx
        sel = jnp.where(hit & (m > NEG_INF), 1.0, sel)
        s = jnp.where(hit, NEG_INF, s)
    return sel


def _moba_prompt_body(q_ref, k_ref, v_ref, o_ref, kmean_ref, kaug_ref, vt_ref, lg_ref):
    i = pl.program_id(2)
    blk = MOBA_BLOCK
    tile = MOBA_KEY_TILE * blk
    T = k_ref.shape[0]
    nb = T // blk
    nbp = kmean_ref.shape[0]
    dh = q_ref.shape[-1]

    @pl.when(i == 0)
    def _():
        kmean_ref[...] = jnp.zeros_like(kmean_ref)
        for jb in range(nb):
            rows = slice(jb * blk, (jb + 1) * blk)
            kmean_ref[jb:jb + 1, :] = jnp.mean(k_ref[rows, :], axis=0, keepdims=True)
            vt_ref[:, rows] = v_ref[rows, :].T.astype(BF16)
        kaug_ref[:, 0:dh] = k_ref[...].astype(BF16)
        key_blk = lax.broadcasted_iota(jnp.int32, (T, LANES), 0) // blk
        on_blk = lax.broadcasted_iota(jnp.int32, (T, LANES), 1) == key_blk
        kaug_ref[:, dh:dh + LANES] = jnp.where(on_blk, MASKED, 0.0).astype(BF16)

    q = q_ref[...]
    s = lax.dot_general(kmean_ref[...], q, NT_DIMS, precision=lax.Precision.HIGHEST, preferred_element_type=F32)
    blk_id = lax.broadcasted_iota(jnp.int32, (nbp, blk), 0)
    sel = _top_blocks(jnp.where(blk_id < i, s, NEG_INF), blk_id, 0)
    qst = (q * (dh ** -0.5)).T.astype(BF16)
    unpicked = jnp.concatenate([1.0 - sel, jnp.zeros((LANES - nbp, blk), F32)], axis=0).astype(BF16)
    qaug = jnp.concatenate([qst, unpicked], axis=0)

    own = pl.multiple_of(i * blk, blk)
    lg_own = jnp.dot(kaug_ref[pl.ds(own, blk), 0:dh], qst, preferred_element_type=F32)
    visible = lax.broadcasted_iota(jnp.int32, (blk, blk), 0) <= lax.broadcasted_iota(jnp.int32, (blk, blk), 1)
    lg_own = jnp.where(visible, lg_own, NEG_INF)
    lg_ref[T:T + blk, :] = lg_own
    n_tiles = (i + MOBA_KEY_TILE - 1) // MOBA_KEY_TILE

    def score(t, m):
        start = pl.multiple_of(t * tile, tile)
        lg = jnp.dot(kaug_ref[pl.ds(start, tile), :], qaug, preferred_element_type=F32)
        lg_ref[pl.ds(start, tile), :] = lg
        return jnp.maximum(m, jnp.max(lg, axis=0, keepdims=True))

    n_pairs = n_tiles // 2
    pair = lambda body: (lambda tp, carry: body(2 * tp + 1, body(2 * tp, carry)))
    m = lax.fori_loop(0, n_pairs, pair(score), jnp.max(lg_own, axis=0, keepdims=True))
    m = lax.fori_loop(2 * n_pairs, n_tiles, score, m)

    p_own = jnp.exp(lg_ref[T:T + blk, :] - m)
    acc0 = jnp.dot(vt_ref[:, pl.ds(own, blk)], p_own.astype(BF16), preferred_element_type=F32)

    def gather(t, carry):
        l, acc = carry
        start = pl.multiple_of(t * tile, tile)
        p = jnp.exp(lg_ref[pl.ds(start, tile), :] - m)
        acc = acc + jnp.dot(vt_ref[:, pl.ds(start, tile)], p.astype(BF16), preferred_element_type=F32)
        return l + jnp.sum(p, axis=0, keepdims=True), acc

    carry = lax.fori_loop(0, n_pairs, pair(gather), (jnp.sum(p_own, axis=0, keepdims=True), acc0))
    l, acc = lax.fori_loop(2 * n_pairs, n_tiles, gather, carry)
    o_ref[...] = (acc / l).T


def moba_prompt(proj, *, batch, q_col, k_col, v_col):
    H, dh, blk = MOBA_HEADS, HEAD_DIM, MOBA_BLOCK
    T = proj.shape[0] // batch
    assert T % (MOBA_KEY_TILE * blk) == 0 and T // blk <= LANES
    nq = T // blk
    nbp = -(-nq // 8) * 8
    return pl.pallas_call(
        _moba_prompt_body,
        grid=(batch, H, nq),
        in_specs=[pl.BlockSpec((blk, dh), lambda b, h, i: (b * nq + i, q_col + h)),
                  pl.BlockSpec((T, dh), lambda b, h, i: (b, k_col + h)),
                  pl.BlockSpec((T, dh), lambda b, h, i: (b, v_col + h))],
        out_specs=pl.BlockSpec((blk, dh), lambda b, h, i: (b * nq + i, h)),
        out_shape=jax.ShapeDtypeStruct((batch * T, H * dh), F32),
        scratch_shapes=[pltpu.VMEM((nbp, dh), F32), pltpu.VMEM((T, dh + LANES), BF16),
                        pltpu.VMEM((dh, T), BF16), pltpu.VMEM((T + blk, blk), F32)],
        compiler_params=_params("parallel", "parallel", "arbitrary"),
        name="moba_prompt",
    )(proj, proj, proj)


def _stack_heads(q):
    heads = q.shape[1] // HEAD_DIM
    return jnp.concatenate([q[:, h * HEAD_DIM:(h + 1) * HEAD_DIM] for h in range(heads)], axis=0)


def _same_head(rows, cols, lq, heads):
    r = lax.broadcasted_iota(jnp.int32, (rows, cols), 0)
    c = lax.broadcasted_iota(jnp.int32, (rows, cols), 1)
    return (r // lq) == (c % heads)


def _moba_pages_body(pt_ref, q_ref, *refs):
    group = MOBA_PAGE_GROUP
    k_refs, v_refs = refs[:group], refs[group:2 * group]
    sc_ref, m_ref, l_ref, o_ref = refs[2 * group:]
    step = pl.program_id(1)
    lq = q_ref.shape[0]
    n_tok, heads, dh = k_refs[0].shape

    @pl.when(step == 0)
    def _():
        sc_ref[...] = jnp.zeros_like(sc_ref)
        m_ref[...] = jnp.zeros_like(m_ref)
        l_ref[...] = jnp.zeros_like(l_ref)

    qa = _stack_heads(q_ref[...])
    rows = qa.shape[0]
    qs = (qa * (dh ** -0.5)).astype(BF16)
    same = _same_head(rows, n_tok * heads, lq, heads)
    lane = lax.broadcasted_iota(jnp.int32, sc_ref.shape, 1)
    sc_all, m_all, l_all = sc_ref[...], m_ref[...], l_ref[...]
    for gi in range(group):
        kp = k_refs[gi][...]
        ksum = jnp.sum(kp, axis=0)
        ksum_rows = jnp.concatenate([jnp.broadcast_to(ksum[h:h + 1, :], (lq, dh)) for h in range(heads)], axis=0)
        sc = jnp.sum(qa * ksum_rows, axis=1, keepdims=True)
        logits = lax.dot_general(qs, kp.reshape(n_tok * heads, dh).astype(BF16), NT_DIMS,
                                 preferred_element_type=F32)
        logits = jnp.where(same, logits, NEG_INF)
        m = jnp.max(logits, axis=1, keepdims=True)
        pr = jnp.exp(logits - m)
        l = jnp.sum(pr, axis=1, keepdims=True)
        o_ref[gi] = _bdot(pr, v_refs[gi][...].reshape(n_tok * heads, dh))
        here = lane == step * group + gi
        sc_all = jnp.where(here, sc, sc_all)
        m_all = jnp.where(here, m, m_all)
        l_all = jnp.where(here, l, l_all)
    sc_ref[...] = sc_all
    m_ref[...] = m_all
    l_ref[...] = l_all


def moba_pages(q, cache_k, cache_v, page_table, layer):
    B, lq, width = q.shape
    _, _, n_tok, heads, dh = cache_k.shape
    n_pages = page_table.shape[1]
    group = MOBA_PAGE_GROUP
    assert n_pages == LANES and heads * dh == width and n_pages % group == 0
    rows = heads * lq
    page_spec = lambda gi: pl.BlockSpec((None, None, n_tok, heads, dh),
                                        lambda b, p, pt: (layer, pt[b, p * group + gi], 0, 0, 0))
    stat_spec = pl.BlockSpec((None, rows, LANES), lambda b, p, pt: (b, 0, 0))
    stat = jax.ShapeDtypeStruct((B, rows, LANES), F32)
    page_specs = [page_spec(gi) for gi in range(group)]
    return pl.pallas_call(
        _moba_pages_body,
        grid_spec=pltpu.PrefetchScalarGridSpec(
            num_scalar_prefetch=1,
            grid=(B, n_pages // group),
            in_specs=[pl.BlockSpec((None, lq, width), lambda b, p, pt: (b, 0, 0))] + page_specs + page_specs,
            out_specs=[stat_spec, stat_spec, stat_spec,
                       pl.BlockSpec((None, group, rows, dh), lambda b, p, pt: (b, p, 0, 0))],
        ),
        out_shape=[stat, stat, stat, jax.ShapeDtypeStruct((B, n_pages, rows, dh), F32)],
        compiler_params=_params("parallel", "arbitrary"),
        name="moba_pages",
    )(page_table, q, *([cache_k] * group), *([cache_v] * group))


def _moba_combine_body(sc_ref, m_ref, l_ref, op_ref, q_ref, kn_ref, vn_ref, o_ref):
    lq = q_ref.shape[0]
    rows = sc_ref.shape[0]
    heads = rows // lq
    n_pages = op_ref.shape[0]
    pages_per_block = MOBA_BLOCK // PAGE_SIZE
    assert pages_per_block == 2
    lane = lax.broadcasted_iota(jnp.int32, (rows, LANES), 1)
    sc = sc_ref[...]
    bs = (sc + pltpu.roll(sc, LANES - 1, 1)) * (1.0 / MOBA_BLOCK)
    sel = _top_blocks(jnp.where(lane % pages_per_block == 0, bs, NEG_INF), lane, 1)
    sel = sel + pltpu.roll(sel, 1, 1)

    qa = _stack_heads(q_ref[...])
    lo = _bdot_nt(qa * (HEAD_DIM ** -0.5), kn_ref[...])
    t_q = lax.broadcasted_iota(jnp.int32, (rows, LANES), 0) % lq
    lo = jnp.where(_same_head(rows, LANES, lq, heads) & (lane // heads <= t_q), lo, NEG_INF)
    m = m_ref[...]
    m_all = jnp.maximum(jnp.max(lo, axis=1, keepdims=True),
                        jnp.max(jnp.where(sel > 0.0, m, NEG_INF), axis=1, keepdims=True))
    w = jnp.where(sel > 0.0, jnp.exp(m - m_all), 0.0)
    p_own = jnp.exp(lo - m_all)
    den = jnp.sum(w * l_ref[...], axis=1, keepdims=True) + jnp.sum(p_own, axis=1, keepdims=True)
    num0 = _bdot(p_own, vn_ref[...])

    num = num0
    for p in range(n_pages):
        num = num + w[:, p:p + 1] * op_ref[p]
    out = num / den
    o_ref[...] = jnp.concatenate([out[h * lq:(h + 1) * lq, :] for h in range(heads)], axis=1)


def moba_combine(sc, m, l, o_pages, q, k_new, v_new):
    B, lq, width = q.shape
    heads, dh = k_new.shape[2:]
    rows = sc.shape[1]
    n_pages = o_pages.shape[1]
    assert LANES % heads == 0 and lq * heads <= LANES
    pad = ((0, 0), (0, LANES // heads - lq), (0, 0), (0, 0))
    kn = jnp.pad(k_new, pad).reshape(B, LANES, dh)
    vn = jnp.pad(v_new, pad).reshape(B, LANES, dh)
    stat_spec = pl.BlockSpec((None, rows, LANES), lambda b: (b, 0, 0))
    new_spec = pl.BlockSpec((None, LANES, dh), lambda b: (b, 0, 0))
    q_spec = pl.BlockSpec((None, lq, width), lambda b: (b, 0, 0))
    return pl.pallas_call(
        _moba_combine_body,
        grid=(B,),
        in_specs=[stat_spec, stat_spec, stat_spec,
                  pl.BlockSpec((None, n_pages, rows, dh), lambda b: (b, 0, 0, 0)),
                  q_spec, new_spec, new_spec],
        out_specs=q_spec,
        out_shape=jax.ShapeDtypeStruct((B, lq, width), F32),
        compiler_params=_params("parallel"),
        name="moba_combine",
    )(sc, m, l, o_pages, q, kn, vn)


def _pad_tokens(a, batch, length):
    a = a.reshape(batch, -1, a.shape[-1])
    return jnp.pad(a, ((0, 0), (0, length - a.shape[1]), (0, 0))).reshape(batch * length, a.shape[-1])


def _unpad_tokens(a, batch, seq):
    return a.reshape(batch, -1, a.shape[-1])[:, :seq].reshape(batch * seq, a.shape[-1])


def _tiles(rows):
    if rows >= 1024:
        return 1024, 1024, 512, 1024, 1280
    return rows, rows, rows, 1024, 1280


def _trunk(x, batch, hgrn_s0, gla_s0, past, lb, wts):
    (norm_w, ffn_gate, ffn_up, ffn_down, even_w_in, even_w_out, hgrn_norm_w, gla_w_in, gla_w_gate,
     gla_b_gate, gla_norm_w, gla_w_out, final_norm_w) = wts
    depth = norm_w.shape[0]
    seq = x.shape[0] // batch
    tm_ffn, tm, tm_out, tn_even, tn_odd = _tiles(x.shape[0])
    padded = seq % REC_CHUNK != 0
    hw = HGRN_HEADS * HEAD_DIM
    mw = MOBA_HEADS * HEAD_DIM
    ks, vs, hs, gs = [], [], [], []
    for l in range(depth):
        x = ffn(x, norm_w[l, 0], ffn_gate, ffn_up, ffn_down, l, 0, tm=tm_ffn)
        if l % 2 == 0:
            e = l // 2
            proj = norm_matmul(x, norm_w[l, 1], even_w_in, e, tm=tm, tn=tn_even)
            rec_in = _pad_tokens(proj[:, :4 * hw], batch, REC_CHUNK) if padded else proj
            o_h, s_h = hgrn_mix(rec_in, lb[e], hgrn_norm_w[e], hgrn_s0[e], batch=batch,
                                valid_len=seq if padded else REC_CHUNK)
            if padded:
                o_h = _unpad_tokens(o_h, batch, seq)
            k_new = proj[:, 4 * hw + mw:4 * hw + 2 * mw]
            v_new = proj[:, 4 * hw + 2 * mw:]
            if past is None:
                first = 4 * hw // HEAD_DIM
                o_m = moba_prompt(proj, batch=batch, q_col=first, k_col=first + MOBA_HEADS,
                                  v_col=first + 2 * MOBA_HEADS)
            else:
                cache_k, cache_v, page_table = past
                q = proj[:, 4 * hw:4 * hw + mw].reshape(batch, seq, mw)
                sc, m, lsum, o_pages = moba_pages(q, cache_k, cache_v, page_table, e)
                heads = lambda a: a.reshape(batch, seq, MOBA_HEADS, HEAD_DIM)
                o_m = moba_combine(sc, m, lsum, o_pages, q, heads(k_new), heads(v_new)).reshape(batch * seq, mw)
            x = out_proj(x, o_h, 0, o_m, 0, even_w_out, e, tm=tm_out)
            ks.append(k_new)
            vs.append(v_new)
            hs.append(s_h)
        else:
            o = l // 2
            proj = norm_matmul(x, norm_w[l, 1], gla_w_in, o, tm=tm, tn=tn_odd)
            rec_in = _pad_tokens(proj, batch, REC_CHUNK) if padded else proj
            o_g, s_g = gla_mix(rec_in, gla_w_gate, o, gla_b_gate[o], gla_norm_w[o], gla_s0[o], batch=batch,
                               valid_len=seq if padded else REC_CHUNK)
            if padded:
                o_g = _unpad_tokens(o_g, batch, seq)
            x = out_proj(x, o_g, 0, o_g, 1, gla_w_out, o, tm=tm_out)
            gs.append(s_g)
        x = ffn(x, norm_w[l, 2], ffn_gate, ffn_up, ffn_down, l, 1,
                final_norm_w if l == depth - 1 else None, tm=tm_ffn)
    return x, ks, vs, hs, gs


def kernel(x_prompt, x_sample, cache_k, cache_v, state_hgrn, state_gla, page_table, norm_w, ffn_gate, ffn_up,
           ffn_down, even_w_in, even_w_out, hgrn_lb_logits, hgrn_norm_w, gla_w_in, gla_w_gate_up, gla_b_gate,
           gla_norm_w, gla_w_out, final_norm_w):
    lb = jnp.cumsum(jax.nn.softmax(hgrn_lb_logits.astype(F32), axis=0), axis=0)
    lb = lb - lb[0:1]

    n_odd, d_model, odd_in = gla_w_in.shape
    gate_col = odd_in - GLA_GATE_RANK
    assert gate_col % LANES == 0
    odd_cols = -(-(gate_col + LANES) // (5 * MXU_DIM)) * (5 * MXU_DIM)
    gla_w_in_p = jnp.pad(gla_w_in, ((0, 0), (0, 0), (0, odd_cols - odd_in))).astype(BF16)
    gla_w_gate_p = jnp.pad(gla_w_gate_up, ((0, 0), (0, LANES - GLA_GATE_RANK), (0, 0))).astype(BF16)
    wts = (norm_w, ffn_gate.astype(BF16), ffn_up.astype(BF16), ffn_down.astype(BF16), even_w_in.astype(BF16),
           even_w_out.astype(BF16), hgrn_norm_w, gla_w_in_p, gla_w_gate_p, gla_b_gate, gla_norm_w,
           gla_w_out.astype(BF16), final_norm_w)

    Bp, Lp, D = x_prompt.shape
    n_even = state_hgrn.shape[0]
    h0 = jnp.zeros((n_even, Bp) + state_hgrn.shape[2:], F32)
    g0 = jnp.zeros((n_odd, Bp) + state_gla.shape[2:], F32)
    y_p, kp, vp, hp, gp = _trunk(x_prompt.reshape(Bp * Lp, D), Bp, h0, g0, None, lb, wts)

    Bs, Ls, _ = x_sample.shape
    y_s, ksn, vsn, hsn, gsn = _trunk(x_sample.reshape(Bs * Ls, D), Bs, state_hgrn, state_gla,
                                     (cache_k, cache_v, page_table), lb, wts)

    pages = lambda a: a.reshape(Bp, Lp // PAGE_SIZE, PAGE_SIZE, MOBA_HEADS, HEAD_DIM)
    rows = lambda a: a.reshape(Bs, Ls, MOBA_HEADS, HEAD_DIM)
    return (y_p.reshape(Bp, Lp, D), y_s.reshape(Bs, Ls, D),
            jnp.stack([pages(a) for a in kp]), jnp.stack([pages(a) for a in vp]),
            jnp.stack(hp), jnp.stack(gp),
            jnp.stack([rows(a) for a in ksn]), jnp.stack([rows(a) for a in vsn]),
            jnp.stack(hsn), jnp.stack(gsn))
```

```python
import functools
from typing import NamedTuple, Optional

import numpy as np
import jax
import jax.numpy as jnp
from jax import lax
from jax.experimental import pallas as pl
from jax.experimental.pallas import tpu as pltpu

F32 = jnp.float32
BF16 = jnp.bfloat16

NORM_EPS = 1e-6
HEAD_DIM = 128
HGRN_HEADS = 8
MOBA_HEADS = 8
MOBA_BLOCK = 256
MOBA_TOPK = 3
PAGE_SIZE = 128
GLA_HEADS = 4
GLA_GATE_RANK = 16
GLA_GATE_NORM = 16.0

LANES = 128
MXU_DIM = 256
REC_CHUNK = 128
REC_LEVELS = 7
REC_SMALL_HALVES = (4, 2)
MOBA_KEY_TILE = 4
MOBA_PAGE_GROUP = 8
MASKED = -1e30
VMEM_LIMIT = 60 * 1024 * 1024

NT_DIMS = (((1,), (1,)), ((), ()))
TN_DIMS = (((0,), (0,)), ((), ()))
NEG_INF = float("-inf")


def _params(*sem):
    return pltpu.CompilerParams(dimension_semantics=sem, vmem_limit_bytes=VMEM_LIMIT)


def _rms(x, w):
    return x * lax.rsqrt(jnp.mean(x * x, axis=-1, keepdims=True) + NORM_EPS) * w


def _silu(x):
    return x * jax.nn.sigmoid(x)


def _bdot(a, b):
    return jnp.dot(a.astype(BF16), b.astype(BF16), preferred_element_type=F32)


def _bdot_nt(a, b):
    return lax.dot_general(a.astype(BF16), b.astype(BF16), NT_DIMS, preferred_element_type=F32)


def _ffn_body(x_ref, nw_ref, wg_ref, wu_ref, wd_ref, *rest, final):
    if final:
        fw_ref, o_ref, h_ref, a_ref = rest
    else:
        o_ref, h_ref, a_ref = rest
    j = pl.program_id(1)
    last = pl.num_programs(1) - 1

    def hidden():
        h = h_ref[...]
        g = jnp.dot(h, wg_ref[...].astype(BF16), preferred_element_type=F32)
        u = jnp.dot(h, wu_ref[...].astype(BF16), preferred_element_type=F32)
        return (_silu(g) * u).astype(BF16)

    def down():
        return jnp.dot(a_ref[...], wd_ref[...].astype(BF16), preferred_element_type=F32)

    @pl.when(j == 0)
    def _():
        h_ref[...] = _rms(x_ref[...], nw_ref[...]).astype(BF16)
        o_ref[...] = jnp.zeros_like(o_ref)
        a_ref[...] = hidden()

    @pl.when((j > 0) & (j < last))
    def _():
        d = down()
        a_ref[...] = hidden()
        o_ref[...] += d

    @pl.when(j == last)
    def _():
        y = x_ref[...] + 0.5 * (o_ref[...] + down())
        if final:
            y = _rms(y, fw_ref[...])
        o_ref[...] = y


def ffn(x, nw, wg, wu, wd, layer, which, final_w=None, *, tm, tf=256):
    M, D = x.shape
    FF = wg.shape[-1]
    assert M % tm == 0 and FF % tf == 0
    nf = FF // tf
    final = final_w is not None
    up_tile = lambda i, j: (layer, which, 0, jnp.minimum(j, nf - 1))
    in_specs = [
        pl.BlockSpec((tm, D), lambda i, j: (i, 0)),
        pl.BlockSpec((1, D), lambda i, j: (0, 0)),
        pl.BlockSpec((None, None, D, tf), up_tile),
        pl.BlockSpec((None, None, D, tf), up_tile),
        pl.BlockSpec((None, None, tf, D), lambda i, j: (layer, which, jnp.maximum(j - 1, 0), 0)),
    ]
    args = [x, nw.reshape(1, D), wg, wu, wd]
    if final:
        in_specs.append(pl.BlockSpec((1, D), lambda i, j: (0, 0)))
        args.append(final_w.reshape(1, D))
    return pl.pallas_call(
        functools.partial(_ffn_body, final=final),
        grid=(M // tm, nf + 1),
        in_specs=in_specs,
        out_specs=pl.BlockSpec((tm, D), lambda i, j: (i, 0)),
        out_shape=jax.ShapeDtypeStruct((M, D), F32),
        scratch_shapes=[pltpu.VMEM((tm, D), BF16), pltpu.VMEM((tm, tf), BF16)],
        compiler_params=_params("parallel", "arbitrary"),
        name="ffn",
    )(*args)


def _norm_mm_body(x_ref, nw_ref, w_ref, o_ref, h_ref):
    @pl.when(pl.program_id(1) == 0)
    def _():
        h_ref[...] = _rms(x_ref[...], nw_ref[...]).astype(BF16)

    o_ref[...] = jnp.dot(h_ref[...], w_ref[...], preferred_element_type=F32)


def norm_matmul(x, nw, w, layer, *, tm, tn):
    M, D = x.shape
    N = w.shape[-1]
    assert M % tm == 0 and N % tn == 0
    return pl.pallas_call(
        _norm_mm_body,
        grid=(M // tm, N // tn),
        in_specs=[
            pl.BlockSpec((tm, D), lambda i, j: (i, 0)),
            pl.BlockSpec((1, D), lambda i, j: (0, 0)),
            pl.BlockSpec((None, D, tn), lambda i, j: (layer, 0, j)),
        ],
        out_specs=pl.BlockSpec((tm, tn), lambda i, j: (i, j)),
        out_shape=jax.ShapeDtypeStruct((M, N), F32),
        scratch_shapes=[pltpu.VMEM((tm, D), BF16)],
        compiler_params=_params("parallel", "arbitrary"),
        name="norm_matmul",
    )(x, nw.reshape(1, D), w)


def _out_proj_body(x_ref, a_ref, b_ref, wa_ref, wb_ref, o_ref):
    o_ref[...] = x_ref[...] + _bdot(a_ref[...], wa_ref[...]) + _bdot(b_ref[...], wb_ref[...])


def out_proj(x, a, a_blk, b, b_blk, w, layer, *, tm, row0):
    D = x.shape[1]
    rows = a.shape[0]
    K = w.shape[1] // 2
    assert rows % tm == 0 and row0 % tm == 0
    first = row0 // tm
    return pl.pallas_call(
        _out_proj_body,
        grid=(rows // tm,),
        in_specs=[
            pl.BlockSpec((tm, D), lambda i: (first + i, 0)),
            pl.BlockSpec((tm, K), lambda i: (i, a_blk)),
            pl.BlockSpec((tm, K), lambda i: (i, b_blk)),
            pl.BlockSpec((None, K, D), lambda i: (layer, 0, 0)),
            pl.BlockSpec((None, K, D), lambda i: (layer, 1, 0)),
        ],
        out_specs=pl.BlockSpec((tm, D), lambda i: (first + i, 0)),
        out_shape=jax.ShapeDtypeStruct(x.shape, F32),
        input_output_aliases={0: 0},
        compiler_params=_params("parallel"),
        name="out_proj",
    )(x, a, b, w, w)


def _rec_constants():
    C = REC_CHUNK
    t = np.arange(C)[:, None]
    s = np.arange(C)[None, :]
    mats = [(s <= t).astype(np.float32)]
    a_mats, b_mats = [], []
    level = np.full((C, C), -1, np.int32)
    level[np.arange(C), np.arange(C)] = REC_LEVELS
    for li in range(REC_LEVELS):
        L = C >> (li + 1)
        mid = (t // (2 * L)) * (2 * L) + L
        if L in REC_SMALL_HALVES:
            a_mats.append(((t >= mid) & (s >= mid) & (s <= t)).astype(np.float32))
            b_mats.append(((t < mid) & (s >= t + 1) & (s <= mid - 1)).astype(np.float32))
        mid_s = (s // (2 * L)) * (2 * L) + L
        level[(t >= mid) & (s < mid_s) & (mid_s == mid)] = li
    w = np.concatenate(mats + a_mats + b_mats, axis=0)
    return jnp.asarray(w, BF16), jnp.asarray(level)


def _level_exponents(li, b, g, e):
    C = REC_CHUNK
    L = C >> (li + 1)
    if L == 1:
        return g, None
    if L in REC_SMALL_HALVES:
        i = REC_SMALL_HALVES.index(L)
        n = len(REC_SMALL_HALVES)
        return e[(1 + i) * C:(2 + i) * C], e[(1 + n + i) * C:(2 + n + i) * C]
    c = jnp.concatenate([jnp.broadcast_to(b[m + L - 1:m + L, :], (2 * L, b.shape[1])) for m in range(0, C, 2 * L)],
                        axis=0)
    d = b - c
    return jnp.minimum(d, 0.0), jnp.minimum(-d, 0.0)


def _rec_core(q, k, vs, g, w_ref, level, st_ref):
    C = REC_CHUNK
    heads = len(vs)
    dk = q.shape[1] // heads
    head = lambda a, h: a[:, h * dk:(h + 1) * dk]
    g1 = g.astype(BF16)
    r1 = g - g1.astype(F32)
    g2 = r1.astype(BF16)
    g3 = (r1 - g2.astype(F32)).astype(BF16)
    w = w_ref[...]
    e = (jnp.dot(w, g1, preferred_element_type=F32) + jnp.dot(w, g2, preferred_element_type=F32)
         + jnp.dot(w, g3, preferred_element_type=F32))
    b = e[0:C]
    b_last = b[C - 1:C, :]
    sts = [st_ref[h] for h in range(heads)]
    qb = (q * jnp.exp(b)).astype(BF16)
    inters = [lax.dot_general(head(qb, h), sts[h].astype(BF16), NT_DIMS, preferred_element_type=F32)
              for h in range(heads)]
    atts = [jnp.where(level == REC_LEVELS, _bdot_nt(head(q, h), head(k, h)), 0.0) for h in range(heads)]
    for li in range(REC_LEVELS):
        eq, ek = _level_exponents(li, b, g, e)
        ql = (q * jnp.exp(eq)).astype(BF16)
        kl = (k if ek is None else k * jnp.exp(ek)).astype(BF16)
        for h in range(heads):
            part = lax.dot_general(head(ql, h), head(kl, h), NT_DIMS, preferred_element_type=F32)
            atts[h] = atts[h] + jnp.where(level == li, part, 0.0)
    kd = (k * jnp.exp(b_last - b)).astype(BF16)
    decay = jnp.exp(b_last)
    outs = []
    for h in range(heads):
        outs.append(inters[h] + _bdot(atts[h], vs[h]))
        upd = lax.dot_general(vs[h].astype(BF16), head(kd, h), TN_DIMS, preferred_element_type=F32)
        st_ref[h] = sts[h] * head(decay, h) + upd
    return outs


def _valid_rows(shape, valid_len):
    return lax.broadcasted_iota(jnp.int32, shape, 0) < valid_len


def _load_state(s0_ref, st_ref):
    @pl.when(pl.program_id(2) == 0)
    def _():
        for hh in range(st_ref.shape[0]):
            st_ref[hh] = s0_ref[hh].T


def _store_state(so_ref, st_ref):
    @pl.when(pl.program_id(2) == pl.num_programs(2) - 1)
    def _():
        for hh in range(st_ref.shape[0]):
            so_ref[hh] = st_ref[hh].T


def _hgrn_body(hq_ref, hf_ref, hi_ref, hg_ref, lb_ref, nw_ref, s0_ref, w_ref, lvl_ref, o_ref, so_ref, st_ref,
               *, valid_len):
    _load_state(s0_ref, st_ref)
    heads, dv, dk = st_ref.shape
    lb = lb_ref[...]
    f = lb + (1.0 - lb) * jax.nn.sigmoid(hf_ref[...])
    k = 1.0 - f
    g = jnp.log(f)
    if valid_len < REC_CHUNK:
        ok = _valid_rows(g.shape, valid_len)
        k = jnp.where(ok, k, 0.0)
        g = jnp.where(ok, g, 0.0)
    q = _silu(hq_ref[...]) * (dk ** -0.5)
    hs = range(heads)
    outs = _rec_core(q, k, [hi_ref[:, h * dv:(h + 1) * dv] for h in hs], g, w_ref, lvl_ref[...], st_ref)
    for h in hs:
        o_ref[:, h * dv:(h + 1) * dv] = _rms(outs[h], nw_ref[...]) * _silu(hg_ref[:, h * dv:(h + 1) * dv])
    _store_state(so_ref, st_ref)


def hgrn_mix(proj, lb, norm_w, s0, *, batch, seq, valid_len=REC_CHUNK, heads_per_step=8):
    H, dk, dv = s0.shape[1:]
    C = REC_CHUNK
    hb = heads_per_step
    assert H % hb == 0
    n = seq // C
    wmat, level = _rec_constants()
    row = lambda b, h, c: b * n + c
    col_spec = lambda grp: pl.BlockSpec((C, hb * dk), lambda b, h, c: (row(b, h, c), grp * (H // hb) + h))
    const = lambda shape: pl.BlockSpec(shape, lambda b, h, c: (0,) * len(shape))
    state_spec = pl.BlockSpec((None, hb, dk, dv), lambda b, h, c: (b, h, 0, 0))
    return pl.pallas_call(
        functools.partial(_hgrn_body, valid_len=valid_len),
        grid=(batch, H // hb, n),
        in_specs=[col_spec(0), col_spec(1), col_spec(2), col_spec(3),
                  pl.BlockSpec((1, hb * dk), lambda b, h, c: (0, h)),
                  const((1, dv)), state_spec, const(wmat.shape), const(level.shape)],
        out_specs=[pl.BlockSpec((C, hb * dv), lambda b, h, c: (row(b, h, c), h)), state_spec],
        out_shape=[jax.ShapeDtypeStruct((batch * n * C, H * dv), F32), jax.ShapeDtypeStruct(s0.shape, F32)],
        scratch_shapes=[pltpu.VMEM((hb, dv, dk), F32)],
        compiler_params=_params("parallel", "parallel", "arbitrary"),
        name="hgrn_mix",
    )(proj, proj, proj, proj, lb.reshape(1, H * dk), norm_w.reshape(1, dv), s0, wmat, level)


def _gla_body(q_ref, k_ref, v_ref, r_ref, a_ref, wg_ref, bg_ref, nw_ref, s0_ref, w_ref, lvl_ref, o_ref, so_ref,
              st_ref, *, valid_len):
    _load_state(s0_ref, st_ref)
    heads, dv, dk = st_ref.shape
    x = _bdot(a_ref[...], wg_ref[...]) + bg_ref[...]
    g = -(jnp.maximum(-x, 0.0) + jnp.log(1.0 + jnp.exp(-jnp.abs(x)))) / GLA_GATE_NORM
    k = k_ref[...]
    if valid_len < REC_CHUNK:
        ok = _valid_rows(g.shape, valid_len)
        k = jnp.where(ok, k, 0.0)
        g = jnp.where(ok, g, 0.0)
    q = q_ref[...] * (dk ** -0.5)
    hs = range(heads)
    outs = _rec_core(q, k, [v_ref[:, h * dv:(h + 1) * dv] for h in hs], g, w_ref, lvl_ref[...], st_ref)
    for h in hs:
        o_ref[:, h * dv:(h + 1) * dv] = _rms(outs[h], nw_ref[...]) * _silu(r_ref[:, h * dv:(h + 1) * dv])
    _store_state(so_ref, st_ref)


def gla_mix(proj, w_gate, layer, b_gate, norm_w, s0, *, batch, seq, valid_len=REC_CHUNK, heads_per_step=4):
    H, dk, dv = s0.shape[1:]
    C = REC_CHUNK
    hb = heads_per_step
    assert H % hb == 0
    n = seq // C
    nh = H // hb
    wmat, level = _rec_constants()
    row = lambda b, h, c: b * n + c
    const = lambda shape: pl.BlockSpec(shape, lambda b, h, c: (0,) * len(shape))
    state_spec = pl.BlockSpec((None, hb, dk, dv), lambda b, h, c: (b, h, 0, 0))
    kv_off = 2 * H * dk // (hb * dv)
    a_blk = (2 * H * dk + 2 * H * dv) // LANES
    return pl.pallas_call(
        functools.partial(_gla_body, valid_len=valid_len),
        grid=(batch, nh, n),
        in_specs=[pl.BlockSpec((C, hb * dk), lambda b, h, c: (row(b, h, c), h)),
                  pl.BlockSpec((C, hb * dk), lambda b, h, c: (row(b, h, c), nh + h)),
                  pl.BlockSpec((C, hb * dv), lambda b, h, c: (row(b, h, c), kv_off + h)),
                  pl.BlockSpec((C, hb * dv), lambda b, h, c: (row(b, h, c), kv_off + nh + h)),
                  pl.BlockSpec((C, LANES), lambda b, h, c: (row(b, h, c), a_blk)),
                  pl.BlockSpec((None, LANES, hb * dk), lambda b, h, c: (layer, 0, h)),
                  pl.BlockSpec((1, hb * dk), lambda b, h, c: (0, h)),
                  const((1, dv)), state_spec, const(wmat.shape), const(level.shape)],
        out_specs=[pl.BlockSpec((C, hb * dv), lambda b, h, c: (row(b, h, c), h)), state_spec],
        out_shape=[jax.ShapeDtypeStruct((batch * n * C, H * dv), F32), jax.ShapeDtypeStruct(s0.shape, F32)],
        scratch_shapes=[pltpu.VMEM((hb, dv, dk), F32)],
        compiler_params=_params("parallel", "parallel", "arbitrary"),
        name="gla_mix",
    )(proj, proj, proj, proj, proj, w_gate, b_gate.reshape(1, H * dk), norm_w.reshape(1, dv), s0, wmat, level)


def _top_blocks(s, index, axis):
    sel = jnp.zeros(s.shape, F32)
    for _ in range(MOBA_TOPK):
        m = jnp.max(s, axis=axis, keepdims=True)
        idx = jnp.min(jnp.where(s == m, index, s.shape[axis]), axis=axis, keepdims=True)
        hit = index == idx
        sel = jnp.where(hit & (m > NEG_INF), 1.0, sel)
        s = jnp.where(hit, NEG_INF, s)
    return sel


def _moba_prompt_body(q_ref, k_ref, v_ref, o_ref, kmean_ref, kaug_ref, vt_ref, lg_ref):
    i = pl.program_id(2)
    blk = MOBA_BLOCK
    tile = MOBA_KEY_TILE * blk
    T = k_ref.shape[0]
    nb = T // blk
    nbp = kmean_ref.shape[0]
    dh = q_ref.shape[-1]

    @pl.when(i == 0)
    def _():
        kmean_ref[...] = jnp.zeros_like(kmean_ref)
        for jb in range(nb):
            rows = slice(jb * blk, (jb + 1) * blk)
            kmean_ref[jb:jb + 1, :] = jnp.mean(k_ref[rows, :], axis=0, keepdims=True)
            vt_ref[:, rows] = v_ref[rows, :].T.astype(BF16)
        kaug_ref[:, 0:dh] = k_ref[...].astype(BF16)
        key_blk = lax.broadcasted_iota(jnp.int32, (T, LANES), 0) // blk
        on_blk = lax.broadcasted_iota(jnp.int32, (T, LANES), 1) == key_blk
        kaug_ref[:, dh:dh + LANES] = jnp.where(on_blk, MASKED, 0.0).astype(BF16)

    q = q_ref[...]
    s = lax.dot_general(kmean_ref[...], q, NT_DIMS, precision=lax.Precision.HIGHEST, preferred_element_type=F32)
    blk_id = lax.broadcasted_iota(jnp.int32, (nbp, blk), 0)
    sel = _top_blocks(jnp.where(blk_id < i, s, NEG_INF), blk_id, 0)
    qst = (q * (dh ** -0.5)).T.astype(BF16)
    unpicked = jnp.concatenate([1.0 - sel, jnp.zeros((LANES - nbp, blk), F32)], axis=0).astype(BF16)
    qaug = jnp.concatenate([qst, unpicked], axis=0)

    own = pl.multiple_of(i * blk, blk)
    lg_own = jnp.dot(kaug_ref[pl.ds(own, blk), 0:dh], qst, preferred_element_type=F32)
    visible = lax.broadcasted_iota(jnp.int32, (blk, blk), 0) <= lax.broadcasted_iota(jnp.int32, (blk, blk), 1)
    lg_own = jnp.where(visible, lg_own, NEG_INF)
    lg_ref[T:T + blk, :] = lg_own
    n_tiles = (i + MOBA_KEY_TILE - 1) // MOBA_KEY_TILE

    def score(t, m):
        start = pl.multiple_of(t * tile, tile)
        lg = jnp.dot(kaug_ref[pl.ds(start, tile), :], qaug, preferred_element_type=F32)
        lg_ref[pl.ds(start, tile), :] = lg
        return jnp.maximum(m, jnp.max(lg, axis=0, keepdims=True))

    n_pairs = n_tiles // 2
    pair = lambda body: (lambda tp, carry: body(2 * tp + 1, body(2 * tp, carry)))
    m = lax.fori_loop(0, n_pairs, pair(score), jnp.max(lg_own, axis=0, keepdims=True))
    m = lax.fori_loop(2 * n_pairs, n_tiles, score, m)

    p_own = jnp.exp(lg_ref[T:T + blk, :] - m)
    acc0 = jnp.dot(vt_ref[:, pl.ds(own, blk)], p_own.astype(BF16), preferred_element_type=F32)

    def gather(t, carry):
        l, acc = carry
        start = pl.multiple_of(t * tile, tile)
        p = jnp.exp(lg_ref[pl.ds(start, tile), :] - m)
        acc = acc + jnp.dot(vt_ref[:, pl.ds(start, tile)], p.astype(BF16), preferred_element_type=F32)
        return l + jnp.sum(p, axis=0, keepdims=True), acc

    carry = lax.fori_loop(0, n_pairs, pair(gather), (jnp.sum(p_own, axis=0, keepdims=True), acc0))
    l, acc = lax.fori_loop(2 * n_pairs, n_tiles, gather, carry)
    o_ref[...] = (acc / l).T


def moba_prompt(proj, *, batch, seq, q_col, k_col, v_col):
    H, dh, blk = MOBA_HEADS, HEAD_DIM, MOBA_BLOCK
    T = seq
    assert T % (MOBA_KEY_TILE * blk) == 0 and T // blk <= LANES
    nq = T // blk
    nbp = -(-nq // 8) * 8
    return pl.pallas_call(
        _moba_prompt_body,
        grid=(batch, H, nq),
        in_specs=[pl.BlockSpec((blk, dh), lambda b, h, i: (b * nq + i, q_col + h)),
                  pl.BlockSpec((T, dh), lambda b, h, i: (b, k_col + h)),
                  pl.BlockSpec((T, dh), lambda b, h, i: (b, v_col + h))],
        out_specs=pl.BlockSpec((blk, dh), lambda b, h, i: (b * nq + i, h)),
        out_shape=jax.ShapeDtypeStruct((batch * T, H * dh), F32),
        scratch_shapes=[pltpu.VMEM((nbp, dh), F32), pltpu.VMEM((T, dh + LANES), BF16),
                        pltpu.VMEM((dh, T), BF16), pltpu.VMEM((T + blk, blk), F32)],
        compiler_params=_params("parallel", "parallel", "arbitrary"),
        name="moba_prompt",
    )(proj, proj, proj)


def _stack_heads(q):
    heads = q.shape[1] // HEAD_DIM
    return jnp.concatenate([q[:, h * HEAD_DIM:(h + 1) * HEAD_DIM] for h in range(heads)], axis=0)


def _same_head(rows, cols, lq, heads):
    r = lax.broadcasted_iota(jnp.int32, (rows, cols), 0)
    c = lax.broadcasted_iota(jnp.int32, (rows, cols), 1)
    return (r // lq) == (c % heads)


def _moba_pages_body(pt_ref, q_ref, *refs):
    group = MOBA_PAGE_GROUP
    k_refs, v_refs = refs[:group], refs[group:2 * group]
    sc_ref, m_ref, l_ref, o_ref = refs[2 * group:]
    step = pl.program_id(1)
    lq = q_ref.shape[0]
    n_tok, heads, dh = k_refs[0].shape

    @pl.when(step == 0)
    def _():
        sc_ref[...] = jnp.zeros_like(sc_ref)
        m_ref[...] = jnp.zeros_like(m_ref)
        l_ref[...] = jnp.zeros_like(l_ref)

    qa = _stack_heads(q_ref[...])
    rows = qa.shape[0]
    qs = (qa * (dh ** -0.5)).astype(BF16)
    same = _same_head(rows, n_tok * heads, lq, heads)
    lane = lax.broadcasted_iota(jnp.int32, sc_ref.shape, 1)
    sc_all, m_all, l_all = sc_ref[...], m_ref[...], l_ref[...]
    for gi in range(group):
        kp = k_refs[gi][...]
        ksum = jnp.sum(kp, axis=0)
        ksum_rows = jnp.concatenate([jnp.broadcast_to(ksum[h:h + 1, :], (lq, dh)) for h in range(heads)], axis=0)
        sc = jnp.sum(qa * ksum_rows, axis=1, keepdims=True)
        logits = lax.dot_general(qs, kp.reshape(n_tok * heads, dh).astype(BF16), NT_DIMS,
                                 preferred_element_type=F32)
        logits = jnp.where(same, logits, NEG_INF)
        m = jnp.max(logits, axis=1, keepdims=True)
        pr = jnp.exp(logits - m)
        l = jnp.sum(pr, axis=1, keepdims=True)
        o_ref[gi] = _bdot(pr, v_refs[gi][...].reshape(n_tok * heads, dh))
        here = lane == step * group + gi
        sc_all = jnp.where(here, sc, sc_all)
        m_all = jnp.where(here, m, m_all)
        l_all = jnp.where(here, l, l_all)
    sc_ref[...] = sc_all
    m_ref[...] = m_all
    l_ref[...] = l_all


def moba_pages(q, cache_k, cache_v, page_table, layer):
    B, lq, width = q.shape
    _, _, n_tok, heads, dh = cache_k.shape
    n_pages = page_table.shape[1]
    group = MOBA_PAGE_GROUP
    assert n_pages == LANES and heads * dh == width and n_pages % group == 0
    rows = heads * lq
    page_spec = lambda gi: pl.BlockSpec((None, None, n_tok, heads, dh),
                                        lambda b, p, pt: (layer, pt[b, p * group + gi], 0, 0, 0))
    stat_spec = pl.BlockSpec((None, rows, LANES), lambda b, p, pt: (b, 0, 0))
    stat = jax.ShapeDtypeStruct((B, rows, LANES), F32)
    page_specs = [page_spec(gi) for gi in range(group)]
    return pl.pallas_call(
        _moba_pages_body,
        grid_spec=pltpu.PrefetchScalarGridSpec(
            num_scalar_prefetch=1,
            grid=(B, n_pages // group),
            in_specs=[pl.BlockSpec((None, lq, width), lambda b, p, pt: (b, 0, 0))] + page_specs + page_specs,
            out_specs=[stat_spec, stat_spec, stat_spec,
                       pl.BlockSpec((None, group, rows, dh), lambda b, p, pt: (b, p, 0, 0))],
        ),
        out_shape=[stat, stat, stat, jax.ShapeDtypeStruct((B, n_pages, rows, dh), F32)],
        compiler_params=_params("parallel", "arbitrary"),
        name="moba_pages",
    )(page_table, q, *([cache_k] * group), *([cache_v] * group))


def _moba_combine_body(sc_ref, m_ref, l_ref, op_ref, q_ref, kn_ref, vn_ref, o_ref):
    lq = q_ref.shape[0]
    rows = sc_ref.shape[0]
    heads = rows // lq
    n_pages = op_ref.shape[0]
    pages_per_block = MOBA_BLOCK // PAGE_SIZE
    assert pages_per_block == 2
    lane = lax.broadcasted_iota(jnp.int32, (rows, LANES), 1)
    sc = sc_ref[...]
    bs = (sc + pltpu.roll(sc, LANES - 1, 1)) * (1.0 / MOBA_BLOCK)
    sel = _top_blocks(jnp.where(lane % pages_per_block == 0, bs, NEG_INF), lane, 1)
    sel = sel + pltpu.roll(sel, 1, 1)

    qa = _stack_heads(q_ref[...])
    lo = _bdot_nt(qa * (HEAD_DIM ** -0.5), kn_ref[...])
    t_q = lax.broadcasted_iota(jnp.int32, (rows, LANES), 0) % lq
    lo = jnp.where(_same_head(rows, LANES, lq, heads) & (lane // heads <= t_q), lo, NEG_INF)
    m = m_ref[...]
    m_all = jnp.maximum(jnp.max(lo, axis=1, keepdims=True),
                        jnp.max(jnp.where(sel > 0.0, m, NEG_INF), axis=1, keepdims=True))
    w = jnp.where(sel > 0.0, jnp.exp(m - m_all), 0.0)
    p_own = jnp.exp(lo - m_all)
    den = jnp.sum(w * l_ref[...], axis=1, keepdims=True) + jnp.sum(p_own, axis=1, keepdims=True)
    num0 = _bdot(p_own, vn_ref[...])

    num = num0
    for p in range(n_pages):
        num = num + w[:, p:p + 1] * op_ref[p]
    out = num / den
    o_ref[...] = jnp.concatenate([out[h * lq:(h + 1) * lq, :] for h in range(heads)], axis=1)


def moba_combine(sc, m, l, o_pages, q, k_new, v_new):
    B, lq, width = q.shape
    heads, dh = k_new.shape[2:]
    rows = sc.shape[1]
    n_pages = o_pages.shape[1]
    assert LANES % heads == 0 and lq * heads <= LANES
    pad = ((0, 0), (0, LANES // heads - lq), (0, 0), (0, 0))
    kn = jnp.pad(k_new, pad).reshape(B, LANES, dh)
    vn = jnp.pad(v_new, pad).reshape(B, LANES, dh)
    stat_spec = pl.BlockSpec((None, rows, LANES), lambda b: (b, 0, 0))
    new_spec = pl.BlockSpec((None, LANES, dh), lambda b: (b, 0, 0))
    q_spec = pl.BlockSpec((None, lq, width), lambda b: (b, 0, 0))
    return pl.pallas_call(
        _moba_combine_body,
        grid=(B,),
        in_specs=[stat_spec, stat_spec, stat_spec,
                  pl.BlockSpec((None, n_pages, rows, dh), lambda b: (b, 0, 0, 0)),
                  q_spec, new_spec, new_spec],
        out_specs=q_spec,
        out_shape=jax.ShapeDtypeStruct((B, lq, width), F32),
        compiler_params=_params("parallel"),
        name="moba_combine",
    )(sc, m, l, o_pages, q, kn, vn)


def _pad_tokens(a, batch, length):
    a = a.reshape(batch, -1, a.shape[-1])
    return jnp.pad(a, ((0, 0), (0, length - a.shape[1]), (0, 0))).reshape(batch * length, a.shape[-1])


def _unpad_tokens(a, batch, seq):
    return a.reshape(batch, -1, a.shape[-1])[:, :seq].reshape(batch * seq, a.shape[-1])


def _row_tile(rows, target):
    n = max(1, round(rows / target))
    assert rows % n == 0 and (rows // n) % 8 == 0, rows
    return rows // n


class _Group(NamedTuple):
    batch: int
    seq: int
    row0: int
    hgrn_s0: jax.Array
    gla_s0: jax.Array
    past: Optional[tuple]


def _group_rows(proj, grp):
    if grp.seq % REC_CHUNK == 0:
        assert grp.row0 == 0
        return proj, grp.seq, REC_CHUNK
    own = proj[grp.row0:grp.row0 + grp.batch * grp.seq]
    seq = -(-grp.seq // REC_CHUNK) * REC_CHUNK
    assert seq == REC_CHUNK
    return _pad_tokens(own, grp.batch, seq), seq, grp.seq


def _even_mixer(proj, grp, e, lb, hgrn_norm_w):
    hw = HGRN_HEADS * HEAD_DIM
    mw = MOBA_HEADS * HEAD_DIM
    rows = grp.batch * grp.seq
    rec_in, seq, valid = _group_rows(proj[:, :4 * hw] if grp.seq % REC_CHUNK else proj, grp)
    o_h, s_h = hgrn_mix(rec_in, lb[e], hgrn_norm_w[e], grp.hgrn_s0[e], batch=grp.batch, seq=seq, valid_len=valid)
    o_h = _unpad_tokens(o_h, grp.batch, grp.seq)
    own = proj[grp.row0:grp.row0 + rows]
    k_new = own[:, 4 * hw + mw:4 * hw + 2 * mw]
    v_new = own[:, 4 * hw + 2 * mw:4 * hw + 3 * mw]
    if grp.past is None:
        assert grp.row0 == 0
        first = 4 * hw // HEAD_DIM
        o_m = moba_prompt(proj, batch=grp.batch, seq=grp.seq, q_col=first, k_col=first + MOBA_HEADS,
                          v_col=first + 2 * MOBA_HEADS)
    else:
        cache_k, cache_v, page_table = grp.past
        q = own[:, 4 * hw:4 * hw + mw].reshape(grp.batch, grp.seq, mw)
        sc, m, lsum, o_pages = moba_pages(q, cache_k, cache_v, page_table, e)
        heads = lambda a: a.reshape(grp.batch, grp.seq, MOBA_HEADS, HEAD_DIM)
        o_m = moba_combine(sc, m, lsum, o_pages, q, heads(k_new), heads(v_new)).reshape(rows, mw)
    return o_h, o_m, s_h, k_new, v_new


def _odd_mixer(proj, grp, o, gla_w_gate, gla_b_gate, gla_norm_w):
    rec_in, seq, valid = _group_rows(proj, grp)
    o_g, s_g = gla_mix(rec_in, gla_w_gate, o, gla_b_gate[o], gla_norm_w[o], grp.gla_s0[o], batch=grp.batch, seq=seq,
                       valid_len=valid)
    return _unpad_tokens(o_g, grp.batch, grp.seq), s_g


def _trunk(x, groups, lb, wts):
    (norm_w, ffn_gate, ffn_up, ffn_down, even_w_in, even_w_out, hgrn_norm_w, gla_w_in, gla_w_gate,
     gla_b_gate, gla_norm_w, gla_w_out, final_norm_w) = wts
    depth = norm_w.shape[0]
    tm = _row_tile(x.shape[0], 1024)
    out_tile = lambda grp: _row_tile(grp.batch * grp.seq, 512)
    ks, vs, hs, gs = ([[] for _ in groups] for _ in range(4))
    for l in range(depth):
        x = ffn(x, norm_w[l, 0], ffn_gate, ffn_up, ffn_down, l, 0, tm=tm)
        if l % 2 == 0:
            e = l // 2
            proj = norm_matmul(x, norm_w[l, 1], even_w_in, e, tm=tm, tn=1024)
            for gi, grp in enumerate(groups):
                o_h, o_m, s_h, k_new, v_new = _even_mixer(proj, grp, e, lb, hgrn_norm_w)
                x = out_proj(x, o_h, 0, o_m, 0, even_w_out, e, tm=out_tile(grp), row0=grp.row0)
                ks[gi].append(k_new)
                vs[gi].append(v_new)
                hs[gi].append(s_h)
        else:
            o = l // 2
            proj = norm_matmul(x, norm_w[l, 1], gla_w_in, o, tm=tm, tn=1280)
            for gi, grp in enumerate(groups):
                o_g, s_g = _odd_mixer(proj, grp, o, gla_w_gate, gla_b_gate, gla_norm_w)
                x = out_proj(x, o_g, 0, o_g, 1, gla_w_out, o, tm=out_tile(grp), row0=grp.row0)
                gs[gi].append(s_g)
        x = ffn(x, norm_w[l, 2], ffn_gate, ffn_up, ffn_down, l, 1, final_norm_w if l == depth - 1 else None, tm=tm)
    return x, ks, vs, hs, gs


def kernel(x_prompt, x_sample, cache_k, cache_v, state_hgrn, state_gla, page_table, norm_w, ffn_gate, ffn_up,
           ffn_down, even_w_in, even_w_out, hgrn_lb_logits, hgrn_norm_w, gla_w_in, gla_w_gate_up, gla_b_gate,
           gla_norm_w, gla_w_out, final_norm_w):
    lb = jnp.cumsum(jax.nn.softmax(hgrn_lb_logits.astype(F32), axis=0), axis=0)
    lb = lb - lb[0:1]

    n_odd, d_model, odd_in = gla_w_in.shape
    gate_col = odd_in - GLA_GATE_RANK
    assert gate_col % LANES == 0
    odd_cols = -(-(gate_col + LANES) // (5 * MXU_DIM)) * (5 * MXU_DIM)
    gla_w_in_p = jnp.pad(gla_w_in, ((0, 0), (0, 0), (0, odd_cols - odd_in))).astype(BF16)
    gla_w_gate_p = jnp.pad(gla_w_gate_up, ((0, 0), (0, LANES - GLA_GATE_RANK), (0, 0))).astype(BF16)
    wts = (norm_w, ffn_gate, ffn_up, ffn_down, even_w_in.astype(BF16), even_w_out.astype(BF16), hgrn_norm_w,
           gla_w_in_p, gla_w_gate_p, gla_b_gate, gla_norm_w, gla_w_out.astype(BF16), final_norm_w)

    Bp, Lp, D = x_prompt.shape
    Bs, Ls, _ = x_sample.shape
    n_even = state_hgrn.shape[0]
    prompt = _Group(Bp, Lp, 0, jnp.zeros((n_even, Bp) + state_hgrn.shape[2:], F32),
                    jnp.zeros((n_odd, Bp) + state_gla.shape[2:], F32), None)
    sample = _Group(Bs, Ls, Bp * Lp, state_hgrn, state_gla, (cache_k, cache_v, page_table))
    total = Bp * Lp + Bs * Ls
    x = jnp.concatenate([x_prompt.reshape(Bp * Lp, D), x_sample.reshape(Bs * Ls, D),
                         jnp.zeros((-total % LANES, D), F32)], axis=0)
    y, ks, vs, hs, gs = _trunk(x, (prompt, sample), lb, wts)

    pages = lambda a: a.reshape(Bp, Lp // PAGE_SIZE, PAGE_SIZE, MOBA_HEADS, HEAD_DIM)
    rows = lambda a: a.reshape(Bs, Ls, MOBA_HEADS, HEAD_DIM)
    return (y[:Bp * Lp].reshape(Bp, Lp, D), y[Bp * Lp:total].reshape(Bs, Ls, D),
            jnp.stack([pages(a) for a in ks[0]]), jnp.stack([pages(a) for a in vs[0]]),
            jnp.stack(hs[0]), jnp.stack(gs[0]),
            jnp.stack([rows(a) for a in ks[1]]), jnp.stack([rows(a) for a in vs[1]]),
            jnp.stack(hs[1]), jnp.stack(gs[1]))
```

```python
import functools
from typing import NamedTuple, Optional

import numpy as np
import jax
import jax.numpy as jnp
from jax import lax
from jax.experimental import pallas as pl
from jax.experimental.pallas import tpu as pltpu

F32 = jnp.float32
BF16 = jnp.bfloat16

NORM_EPS = 1e-6
HEAD_DIM = 128
HGRN_HEADS = 8
MOBA_HEADS = 8
MOBA_BLOCK = 256
MOBA_TOPK = 3
PAGE_SIZE = 128
GLA_HEADS = 4
GLA_GATE_RANK = 16
GLA_GATE_NORM = 16.0

LANES = 128
MXU_DIM = 256
REC_CHUNK = 128
REC_LEVELS = 7
REC_SMALL_HALVES = (4, 2)
MOBA_KEY_TILE = 4
MOBA_PAGE_GROUP = 8
MASKED = -1e30
VMEM_LIMIT = 60 * 1024 * 1024

NT_DIMS = (((1,), (1,)), ((), ()))
TN_DIMS = (((0,), (0,)), ((), ()))
NEG_INF = float("-inf")


def _params(*sem):
    return pltpu.CompilerParams(dimension_semantics=sem, vmem_limit_bytes=VMEM_LIMIT)


def _rms(x, w):
    return x * lax.rsqrt(jnp.mean(x * x, axis=-1, keepdims=True) + NORM_EPS) * w


def _silu(x):
    return x * jax.nn.sigmoid(x)


def _bdot(a, b):
    return jnp.dot(a.astype(BF16), b.astype(BF16), preferred_element_type=F32)


def _bdot_nt(a, b):
    return lax.dot_general(a.astype(BF16), b.astype(BF16), NT_DIMS, preferred_element_type=F32)


def _ffn_body(x_ref, nw_ref, wg_ref, wu_ref, wd_ref, *rest, final):
    if final:
        fw_ref, o_ref, h_ref, a_ref = rest
    else:
        o_ref, h_ref, a_ref = rest
    j = pl.program_id(1)
    last = pl.num_programs(1) - 1

    def hidden():
        h = h_ref[...]
        g = jnp.dot(h, wg_ref[...].astype(BF16), preferred_element_type=F32)
        u = jnp.dot(h, wu_ref[...].astype(BF16), preferred_element_type=F32)
        return (_silu(g) * u).astype(BF16)

    def down():
        return jnp.dot(a_ref[...], wd_ref[...].astype(BF16), preferred_element_type=F32)

    @pl.when(j == 0)
    def _():
        h_ref[...] = _rms(x_ref[...], nw_ref[...]).astype(BF16)
        o_ref[...] = jnp.zeros_like(o_ref)
        a_ref[...] = hidden()

    @pl.when((j > 0) & (j < last))
    def _():
        d = down()
        a_ref[...] = hidden()
        o_ref[...] += d

    @pl.when(j == last)
    def _():
        y = x_ref[...] + 0.5 * (o_ref[...] + down())
        if final:
            y = _rms(y, fw_ref[...])
        o_ref[...] = y


def ffn(x, nw, wg, wu, wd, layer, which, final_w=None, *, tm, tf=256):
    M, D = x.shape
    FF = wg.shape[-1]
    assert M % tm == 0 and FF % tf == 0
    nf = FF // tf
    final = final_w is not None
    up_tile = lambda i, j: (layer, which, 0, jnp.minimum(j, nf - 1))
    in_specs = [
        pl.BlockSpec((tm, D), lambda i, j: (i, 0)),
        pl.BlockSpec((1, D), lambda i, j: (0, 0)),
        pl.BlockSpec((None, None, D, tf), up_tile),
        pl.BlockSpec((None, None, D, tf), up_tile),
        pl.BlockSpec((None, None, tf, D), lambda i, j: (layer, which, jnp.maximum(j - 1, 0), 0)),
    ]
    args = [x, nw.reshape(1, D), wg, wu, wd]
    if final:
        in_specs.append(pl.BlockSpec((1, D), lambda i, j: (0, 0)))
        args.append(final_w.reshape(1, D))
    return pl.pallas_call(
        functools.partial(_ffn_body, final=final),
        grid=(M // tm, nf + 1),
        in_specs=in_specs,
        out_specs=pl.BlockSpec((tm, D), lambda i, j: (i, 0)),
        out_shape=jax.ShapeDtypeStruct((M, D), F32),
        scratch_shapes=[pltpu.VMEM((tm, D), BF16), pltpu.VMEM((tm, tf), BF16)],
        compiler_params=_params("parallel", "arbitrary"),
        name="ffn",
    )(*args)


def _norm_mm_body(x_ref, nw_ref, w_ref, o_ref, h_ref):
    @pl.when(pl.program_id(1) == 0)
    def _():
        h_ref[...] = _rms(x_ref[...], nw_ref[...]).astype(BF16)

    o_ref[...] = jnp.dot(h_ref[...], w_ref[...], preferred_element_type=F32)


def norm_matmul(x, nw, w, layer, *, tm, tn):
    M, D = x.shape
    N = w.shape[-1]
    assert M % tm == 0 and N % tn == 0
    return pl.pallas_call(
        _norm_mm_body,
        grid=(M // tm, N // tn),
        in_specs=[
            pl.BlockSpec((tm, D), lambda i, j: (i, 0)),
            pl.BlockSpec((1, D), lambda i, j: (0, 0)),
            pl.BlockSpec((None, D, tn), lambda i, j: (layer, 0, j)),
        ],
        out_specs=pl.BlockSpec((tm, tn), lambda i, j: (i, j)),
        out_shape=jax.ShapeDtypeStruct((M, N), F32),
        scratch_shapes=[pltpu.VMEM((tm, D), BF16)],
        compiler_params=_params("parallel", "arbitrary"),
        name="norm_matmul",
    )(x, nw.reshape(1, D), w)


def _out_proj_body(x_ref, a_ref, b_ref, wa_ref, wb_ref, o_ref):
    o_ref[...] = x_ref[...] + _bdot(a_ref[...], wa_ref[...]) + _bdot(b_ref[...], wb_ref[...])


def out_proj(x, a, a_blk, b, b_blk, w, layer, *, tm, row0):
    D = x.shape[1]
    rows = a.shape[0]
    K = w.shape[1] // 2
    assert rows % tm == 0 and row0 % tm == 0
    first = row0 // tm
    return pl.pallas_call(
        _out_proj_body,
        grid=(rows // tm,),
        in_specs=[
            pl.BlockSpec((tm, D), lambda i: (first + i, 0)),
            pl.BlockSpec((tm, K), lambda i: (i, a_blk)),
            pl.BlockSpec((tm, K), lambda i: (i, b_blk)),
            pl.BlockSpec((None, K, D), lambda i: (layer, 0, 0)),
            pl.BlockSpec((None, K, D), lambda i: (layer, 1, 0)),
        ],
        out_specs=pl.BlockSpec((tm, D), lambda i: (first + i, 0)),
        out_shape=jax.ShapeDtypeStruct(x.shape, F32),
        input_output_aliases={0: 0},
        compiler_params=_params("parallel"),
        name="out_proj",
    )(x, a, b, w, w)


def _rec_constants():
    C = REC_CHUNK
    t = np.arange(C)[:, None]
    s = np.arange(C)[None, :]
    mats = [(s <= t).astype(np.float32)]
    a_mats, b_mats = [], []
    level = np.full((C, C), -1, np.int32)
    level[np.arange(C), np.arange(C)] = REC_LEVELS
    for li in range(REC_LEVELS):
        L = C >> (li + 1)
        mid = (t // (2 * L)) * (2 * L) + L
        if L in REC_SMALL_HALVES:
            a_mats.append(((t >= mid) & (s >= mid) & (s <= t)).astype(np.float32))
            b_mats.append(((t < mid) & (s >= t + 1) & (s <= mid - 1)).astype(np.float32))
        mid_s = (s // (2 * L)) * (2 * L) + L
        level[(t >= mid) & (s < mid_s) & (mid_s == mid)] = li
    w = np.concatenate(mats + a_mats + b_mats, axis=0)
    return jnp.asarray(w, BF16), jnp.asarray(level)


def _level_exponents(li, b, g, e):
    C = REC_CHUNK
    L = C >> (li + 1)
    if L == 1:
        return g, None
    if L in REC_SMALL_HALVES:
        i = REC_SMALL_HALVES.index(L)
        n = len(REC_SMALL_HALVES)
        return e[(1 + i) * C:(2 + i) * C], e[(1 + n + i) * C:(2 + n + i) * C]
    c = jnp.concatenate([jnp.broadcast_to(b[m + L - 1:m + L, :], (2 * L, b.shape[1])) for m in range(0, C, 2 * L)],
                        axis=0)
    d = b - c
    return jnp.minimum(d, 0.0), jnp.minimum(-d, 0.0)


def _rec_core(q, k, vs, g, w_ref, level, st_ref):
    C = REC_CHUNK
    heads = len(vs)
    dk = q.shape[1] // heads
    head = lambda a, h: a[:, h * dk:(h + 1) * dk]
    g1 = g.astype(BF16)
    r1 = g - g1.astype(F32)
    g2 = r1.astype(BF16)
    g3 = (r1 - g2.astype(F32)).astype(BF16)
    w = w_ref[...]
    e = (jnp.dot(w, g1, preferred_element_type=F32) + jnp.dot(w, g2, preferred_element_type=F32)
         + jnp.dot(w, g3, preferred_element_type=F32))
    b = e[0:C]
    b_last = b[C - 1:C, :]
    sts = [st_ref[h] for h in range(heads)]
    qb = (q * jnp.exp(b)).astype(BF16)
    inters = [lax.dot_general(head(qb, h), sts[h].astype(BF16), NT_DIMS, preferred_element_type=F32)
              for h in range(heads)]
    atts = [jnp.where(level == REC_LEVELS, _bdot_nt(head(q, h), head(k, h)), 0.0) for h in range(heads)]
    for li in range(REC_LEVELS):
        eq, ek = _level_exponents(li, b, g, e)
        ql = (q * jnp.exp(eq)).astype(BF16)
        kl = (k if ek is None else k * jnp.exp(ek)).astype(BF16)
        for h in range(heads):
            part = lax.dot_general(head(ql, h), head(kl, h), NT_DIMS, preferred_element_type=F32)
            atts[h] = atts[h] + jnp.where(level == li, part, 0.0)
    kd = (k * jnp.exp(b_last - b)).astype(BF16)
    decay = jnp.exp(b_last)
    outs = []
    for h in range(heads):
        outs.append(inters[h] + _bdot(atts[h], vs[h]))
        upd = lax.dot_general(vs[h].astype(BF16), head(kd, h), TN_DIMS, preferred_element_type=F32)
        st_ref[h] = sts[h] * head(decay, h) + upd
    return outs


def _valid_rows(shape, valid_len):
    return lax.broadcasted_iota(jnp.int32, shape, 0) < valid_len


def _load_state(s0_ref, st_ref):
    @pl.when(pl.program_id(2) == 0)
    def _():
        for hh in range(st_ref.shape[0]):
            st_ref[hh] = s0_ref[hh].T


def _store_state(so_ref, st_ref):
    @pl.when(pl.program_id(2) == pl.num_programs(2) - 1)
    def _():
        for hh in range(st_ref.shape[0]):
            so_ref[hh] = st_ref[hh].T


def _hgrn_body(hq_ref, hf_ref, hi_ref, hg_ref, lb_ref, nw_ref, s0_ref, w_ref, lvl_ref, o_ref, so_ref, st_ref,
               *, valid_len):
    _load_state(s0_ref, st_ref)
    heads, dv, dk = st_ref.shape
    lb = lb_ref[...]
    f = lb + (1.0 - lb) * jax.nn.sigmoid(hf_ref[...])
    k = 1.0 - f
    g = jnp.log(f)
    if valid_len < REC_CHUNK:
        ok = _valid_rows(g.shape, valid_len)
        k = jnp.where(ok, k, 0.0)
        g = jnp.where(ok, g, 0.0)
    q = _silu(hq_ref[...]) * (dk ** -0.5)
    hs = range(heads)
    outs = _rec_core(q, k, [hi_ref[:, h * dv:(h + 1) * dv] for h in hs], g, w_ref, lvl_ref[...], st_ref)
    for h in hs:
        o_ref[:, h * dv:(h + 1) * dv] = _rms(outs[h], nw_ref[...]) * _silu(hg_ref[:, h * dv:(h + 1) * dv])
    _store_state(so_ref, st_ref)


def hgrn_mix(proj, lb, norm_w, s0, *, batch, seq, valid_len=REC_CHUNK, heads_per_step=8):
    H, dk, dv = s0.shape[1:]
    C = REC_CHUNK
    hb = heads_per_step
    assert H % hb == 0
    n = seq // C
    wmat, level = _rec_constants()
    row = lambda b, h, c: b * n + c
    col_spec = lambda grp: pl.BlockSpec((C, hb * dk), lambda b, h, c: (row(b, h, c), grp * (H // hb) + h))
    const = lambda shape: pl.BlockSpec(shape, lambda b, h, c: (0,) * len(shape))
    state_spec = pl.BlockSpec((None, hb, dk, dv), lambda b, h, c: (b, h, 0, 0))
    return pl.pallas_call(
        functools.partial(_hgrn_body, valid_len=valid_len),
        grid=(batch, H // hb, n),
        in_specs=[col_spec(0), col_spec(1), col_spec(2), col_spec(3),
                  pl.BlockSpec((1, hb * dk), lambda b, h, c: (0, h)),
                  const((1, dv)), state_spec, const(wmat.shape), const(level.shape)],
        out_specs=[pl.BlockSpec((C, hb * dv), lambda b, h, c: (row(b, h, c), h)), state_spec],
        out_shape=[jax.ShapeDtypeStruct((batch * n * C, H * dv), F32), jax.ShapeDtypeStruct(s0.shape, F32)],
        scratch_shapes=[pltpu.VMEM((hb, dv, dk), F32)],
        compiler_params=_params("parallel", "parallel", "arbitrary"),
        name="hgrn_mix",
    )(proj, proj, proj, proj, lb.reshape(1, H * dk), norm_w.reshape(1, dv), s0, wmat, level)


def _gla_body(q_ref, k_ref, v_ref, r_ref, a_ref, wg_ref, bg_ref, nw_ref, s0_ref, w_ref, lvl_ref, o_ref, so_ref,
              st_ref, *, valid_len):
    _load_state(s0_ref, st_ref)
    heads, dv, dk = st_ref.shape
    x = _bdot(a_ref[...], wg_ref[...]) + bg_ref[...]
    g = -(jnp.maximum(-x, 0.0) + jnp.log(1.0 + jnp.exp(-jnp.abs(x)))) / GLA_GATE_NORM
    k = k_ref[...]
    if valid_len < REC_CHUNK:
        ok = _valid_rows(g.shape, valid_len)
        k = jnp.where(ok, k, 0.0)
        g = jnp.where(ok, g, 0.0)
    q = q_ref[...] * (dk ** -0.5)
    hs = range(heads)
    outs = _rec_core(q, k, [v_ref[:, h * dv:(h + 1) * dv] for h in hs], g, w_ref, lvl_ref[...], st_ref)
    for h in hs:
        o_ref[:, h * dv:(h + 1) * dv] = _rms(outs[h], nw_ref[...]) * _silu(r_ref[:, h * dv:(h + 1) * dv])
    _store_state(so_ref, st_ref)


def gla_mix(proj, w_gate, layer, b_gate, norm_w, s0, *, batch, seq, valid_len=REC_CHUNK, heads_per_step=4):
    H, dk, dv = s0.shape[1:]
    C = REC_CHUNK
    hb = heads_per_step
    assert H % hb == 0
    n = seq // C
    nh = H // hb
    wmat, level = _rec_constants()
    row = lambda b, h, c: b * n + c
    const = lambda shape: pl.BlockSpec(shape, lambda b, h, c: (0,) * len(shape))
    state_spec = pl.BlockSpec((None, hb, dk, dv), lambda b, h, c: (b, h, 0, 0))
    kv_off = 2 * H * dk // (hb * dv)
    a_blk = (2 * H * dk + 2 * H * dv) // LANES
    return pl.pallas_call(
        functools.partial(_gla_body, valid_len=valid_len),
        grid=(batch, nh, n),
        in_specs=[pl.BlockSpec((C, hb * dk), lambda b, h, c: (row(b, h, c), h)),
                  pl.BlockSpec((C, hb * dk), lambda b, h, c: (row(b, h, c), nh + h)),
                  pl.BlockSpec((C, hb * dv), lambda b, h, c: (row(b, h, c), kv_off + h)),
                  pl.BlockSpec((C, hb * dv), lambda b, h, c: (row(b, h, c), kv_off + nh + h)),
                  pl.BlockSpec((C, LANES), lambda b, h, c: (row(b, h, c), a_blk)),
                  pl.BlockSpec((None, LANES, hb * dk), lambda b, h, c: (layer, 0, h)),
                  pl.BlockSpec((1, hb * dk), lambda b, h, c: (0, h)),
                  const((1, dv)), state_spec, const(wmat.shape), const(level.shape)],
        out_specs=[pl.BlockSpec((C, hb * dv), lambda b, h, c: (row(b, h, c), h)), state_spec],
        out_shape=[jax.ShapeDtypeStruct((batch * n * C, H * dv), F32), jax.ShapeDtypeStruct(s0.shape, F32)],
        scratch_shapes=[pltpu.VMEM((hb, dv, dk), F32)],
        compiler_params=_params("parallel", "parallel", "arbitrary"),
        name="gla_mix",
    )(proj, proj, proj, proj, proj, w_gate, b_gate.reshape(1, H * dk), norm_w.reshape(1, dv), s0, wmat, level)


def _top_blocks(s, index, axis):
    sel = jnp.zeros(s.shape, F32)
    for _ in range(MOBA_TOPK):
        m = jnp.max(s, axis=axis, keepdims=True)
        idx = jnp.min(jnp.where(s == m, index, s.shape[axis]), axis=axis, keepdims=True)
        hit = index == idx
        sel = jnp.where(hit & (m > NEG_INF), 1.0, sel)
        s = jnp.where(hit, NEG_INF, s)
    return sel


def _moba_prompt_body(q_ref, k_ref, v_ref, o_ref, kmean_ref, kaug_ref, vt_ref, lg_ref):
    i = pl.program_id(2)
    blk = MOBA_BLOCK
    tile = MOBA_KEY_TILE * blk
    T = k_ref.shape[0]
    nb = T // blk
    nbp = kmean_ref.shape[0]
    dh = q_ref.shape[-1]

    @pl.when(i == 0)
    def _():
        kmean_ref[...] = jnp.zeros_like(kmean_ref)
        for jb in range(nb):
            rows = slice(jb * blk, (jb + 1) * blk)
            kmean_ref[jb:jb + 1, :] = jnp.mean(k_ref[rows, :], axis=0, keepdims=True)
            vt_ref[:, rows] = v_ref[rows, :].T.astype(BF16)
        kaug_ref[:, 0:dh] = k_ref[...].astype(BF16)
        key_blk = lax.broadcasted_iota(jnp.int32, (T, LANES), 0) // blk
        on_blk = lax.broadcasted_iota(jnp.int32, (T, LANES), 1) == key_blk
        kaug_ref[:, dh:dh + LANES] = jnp.where(on_blk, MASKED, 0.0).astype(BF16)

    q = q_ref[...]
    s = lax.dot_general(kmean_ref[...], q, NT_DIMS, precision=lax.Precision.HIGHEST, preferred_element_type=F32)
    blk_id = lax.broadcasted_iota(jnp.int32, (nbp, blk), 0)
    sel = _top_blocks(jnp.where(blk_id < i, s, NEG_INF), blk_id, 0)
    qst = (q * (dh ** -0.5)).T.astype(BF16)
    unpicked = jnp.concatenate([1.0 - sel, jnp.zeros((LANES - nbp, blk), F32)], axis=0).astype(BF16)
    qaug = jnp.concatenate([qst, unpicked], axis=0)

    own = pl.multiple_of(i * blk, blk)
    lg_own = jnp.dot(kaug_ref[pl.ds(own, blk), 0:dh], qst, preferred_element_type=F32)
    visible = lax.broadcasted_iota(jnp.int32, (blk, blk), 0) <= lax.broadcasted_iota(jnp.int32, (blk, blk), 1)
    lg_own = jnp.where(visible, lg_own, NEG_INF)
    lg_ref[T:T + blk, :] = lg_own
    n_tiles = (i + MOBA_KEY_TILE - 1) // MOBA_KEY_TILE

    def score(t, m):
        start = pl.multiple_of(t * tile, tile)
        lg = jnp.dot(kaug_ref[pl.ds(start, tile), :], qaug, preferred_element_type=F32)
        lg_ref[pl.ds(start, tile), :] = lg
        return jnp.maximum(m, jnp.max(lg, axis=0, keepdims=True))

    n_pairs = n_tiles // 2
    pair = lambda body: (lambda tp, carry: body(2 * tp + 1, body(2 * tp, carry)))
    m = lax.fori_loop(0, n_pairs, pair(score), jnp.max(lg_own, axis=0, keepdims=True))
    m = lax.fori_loop(2 * n_pairs, n_tiles, score, m)

    p_own = jnp.exp(lg_ref[T:T + blk, :] - m)
    acc0 = jnp.dot(vt_ref[:, pl.ds(own, blk)], p_own.astype(BF16), preferred_element_type=F32)

    def gather(t, carry):
        l, acc = carry
        start = pl.multiple_of(t * tile, tile)
        p = jnp.exp(lg_ref[pl.ds(start, tile), :] - m)
        acc = acc + jnp.dot(vt_ref[:, pl.ds(start, tile)], p.astype(BF16), preferred_element_type=F32)
        return l + jnp.sum(p, axis=0, keepdims=True), acc

    carry = lax.fori_loop(0, n_pairs, pair(gather), (jnp.sum(p_own, axis=0, keepdims=True), acc0))
    l, acc = lax.fori_loop(2 * n_pairs, n_tiles, gather, carry)
    o_ref[...] = (acc / l).T


def moba_prompt(proj, *, batch, seq, q_col, k_col, v_col):
    H, dh, blk = MOBA_HEADS, HEAD_DIM, MOBA_BLOCK
    T = seq
    assert T % (MOBA_KEY_TILE * blk) == 0 and T // blk <= LANES
    nq = T // blk
    nbp = -(-nq // 8) * 8
    return pl.pallas_call(
        _moba_prompt_body,
        grid=(batch, H, nq),
        in_specs=[pl.BlockSpec((blk, dh), lambda b, h, i: (b * nq + i, q_col + h)),
                  pl.BlockSpec((T, dh), lambda b, h, i: (b, k_col + h)),
                  pl.BlockSpec((T, dh), lambda b, h, i: (b, v_col + h))],
        out_specs=pl.BlockSpec((blk, dh), lambda b, h, i: (b * nq + i, h)),
        out_shape=jax.ShapeDtypeStruct((batch * T, H * dh), F32),
        scratch_shapes=[pltpu.VMEM((nbp, dh), F32), pltpu.VMEM((T, dh + LANES), BF16),
                        pltpu.VMEM((dh, T), BF16), pltpu.VMEM((T + blk, blk), F32)],
        compiler_params=_params("parallel", "parallel", "arbitrary"),
        name="moba_prompt",
    )(proj, proj, proj)


def _stack_heads(q):
    heads = q.shape[1] // HEAD_DIM
    return jnp.concatenate([q[:, h * HEAD_DIM:(h + 1) * HEAD_DIM] for h in range(heads)], axis=0)


def _same_head(rows, cols, lq, heads):
    r = lax.broadcasted_iota(jnp.int32, (rows, cols), 0)
    c = lax.broadcasted_iota(jnp.int32, (rows, cols), 1)
    return (r // lq) == (c % heads)


def _moba_pages_body(pt_ref, q_ref, *refs):
    group = MOBA_PAGE_GROUP
    k_refs, v_refs = refs[:group], refs[group:2 * group]
    sc_ref, m_ref, l_ref, o_ref = refs[2 * group:]
    step = pl.program_id(1)
    lq = q_ref.shape[0]
    n_tok, heads, dh = k_refs[0].shape

    @pl.when(step == 0)
    def _():
        sc_ref[...] = jnp.zeros_like(sc_ref)
        m_ref[...] = jnp.zeros_like(m_ref)
        l_ref[...] = jnp.zeros_like(l_ref)

    qa = _stack_heads(q_ref[...])
    rows = qa.shape[0]
    qs = (qa * (dh ** -0.5)).astype(BF16)
    same = _same_head(rows, n_tok * heads, lq, heads)
    lane = lax.broadcasted_iota(jnp.int32, sc_ref.shape, 1)
    sc_all, m_all, l_all = sc_ref[...], m_ref[...], l_ref[...]
    pages = range(group)
    logits = [lax.dot_general(qs, k_refs[gi][...].reshape(n_tok * heads, dh).astype(BF16), NT_DIMS,
                              preferred_element_type=F32) for gi in pages]
    logits = [jnp.where(same, lg, NEG_INF) for lg in logits]
    ms = [jnp.max(lg, axis=1, keepdims=True) for lg in logits]
    prs = [jnp.exp(logits[gi] - ms[gi]) for gi in pages]
    for gi in pages:
        o_ref[gi] = _bdot(prs[gi], v_refs[gi][...].reshape(n_tok * heads, dh))
    for gi in pages:
        ksum = jnp.sum(k_refs[gi][...], axis=0)
        ksum_rows = jnp.concatenate([jnp.broadcast_to(ksum[h:h + 1, :], (lq, dh)) for h in range(heads)], axis=0)
        here = lane == step * group + gi
        sc_all = jnp.where(here, jnp.sum(qa * ksum_rows, axis=1, keepdims=True), sc_all)
        m_all = jnp.where(here, ms[gi], m_all)
        l_all = jnp.where(here, jnp.sum(prs[gi], axis=1, keepdims=True), l_all)
    sc_ref[...] = sc_all
    m_ref[...] = m_all
    l_ref[...] = l_all


def moba_pages(q, cache_k, cache_v, page_table, layer):
    B, lq, width = q.shape
    _, _, n_tok, heads, dh = cache_k.shape
    n_pages = page_table.shape[1]
    group = MOBA_PAGE_GROUP
    assert n_pages == LANES and heads * dh == width and n_pages % group == 0
    rows = heads * lq
    page_spec = lambda gi: pl.BlockSpec((None, None, n_tok, heads, dh),
                                        lambda b, p, pt: (layer, pt[b, p * group + gi], 0, 0, 0))
    stat_spec = pl.BlockSpec((None, rows, LANES), lambda b, p, pt: (b, 0, 0))
    stat = jax.ShapeDtypeStruct((B, rows, LANES), F32)
    page_specs = [page_spec(gi) for gi in range(group)]
    return pl.pallas_call(
        _moba_pages_body,
        grid_spec=pltpu.PrefetchScalarGridSpec(
            num_scalar_prefetch=1,
            grid=(B, n_pages // group),
            in_specs=[pl.BlockSpec((None, lq, width), lambda b, p, pt: (b, 0, 0))] + page_specs + page_specs,
            out_specs=[stat_spec, stat_spec, stat_spec,
                       pl.BlockSpec((None, group, rows, dh), lambda b, p, pt: (b, p, 0, 0))],
        ),
        out_shape=[stat, stat, stat, jax.ShapeDtypeStruct((B, n_pages, rows, dh), F32)],
        compiler_params=_params("parallel", "arbitrary"),
        name="moba_pages",
    )(page_table, q, *([cache_k] * group), *([cache_v] * group))


def _moba_combine_body(sc_ref, m_ref, l_ref, op_ref, q_ref, kn_ref, vn_ref, o_ref):
    lq = q_ref.shape[0]
    rows = sc_ref.shape[0]
    heads = rows // lq
    n_pages = op_ref.shape[0]
    pages_per_block = MOBA_BLOCK // PAGE_SIZE
    assert pages_per_block == 2
    lane = lax.broadcasted_iota(jnp.int32, (rows, LANES), 1)
    sc = sc_ref[...]
    bs = (sc + pltpu.roll(sc, LANES - 1, 1)) * (1.0 / MOBA_BLOCK)
    sel = _top_blocks(jnp.where(lane % pages_per_block == 0, bs, NEG_INF), lane, 1)
    sel = sel + pltpu.roll(sel, 1, 1)

    qa = _stack_heads(q_ref[...])
    lo = _bdot_nt(qa * (HEAD_DIM ** -0.5), kn_ref[...])
    t_q = lax.broadcasted_iota(jnp.int32, (rows, LANES), 0) % lq
    lo = jnp.where(_same_head(rows, LANES, lq, heads) & (lane // heads <= t_q), lo, NEG_INF)
    m = m_ref[...]
    m_all = jnp.maximum(jnp.max(lo, axis=1, keepdims=True),
                        jnp.max(jnp.where(sel > 0.0, m, NEG_INF), axis=1, keepdims=True))
    w = jnp.where(sel > 0.0, jnp.exp(m - m_all), 0.0)
    p_own = jnp.exp(lo - m_all)
    den = jnp.sum(w * l_ref[...], axis=1, keepdims=True) + jnp.sum(p_own, axis=1, keepdims=True)
    num0 = _bdot(p_own, vn_ref[...])

    num = num0
    for p in range(n_pages):
        num = num + w[:, p:p + 1] * op_ref[p]
    out = num / den
    o_ref[...] = jnp.concatenate([out[h * lq:(h + 1) * lq, :] for h in range(heads)], axis=1)


def moba_combine(sc, m, l, o_pages, q, k_new, v_new):
    B, lq, width = q.shape
    heads, dh = k_new.shape[2:]
    rows = sc.shape[1]
    n_pages = o_pages.shape[1]
    assert LANES % heads == 0 and lq * heads <= LANES
    pad = ((0, 0), (0, LANES // heads - lq), (0, 0), (0, 0))
    kn = jnp.pad(k_new, pad).reshape(B, LANES, dh)
    vn = jnp.pad(v_new, pad).reshape(B, LANES, dh)
    stat_spec = pl.BlockSpec((None, rows, LANES), lambda b: (b, 0, 0))
    new_spec = pl.BlockSpec((None, LANES, dh), lambda b: (b, 0, 0))
    q_spec = pl.BlockSpec((None, lq, width), lambda b: (b, 0, 0))
    return pl.pallas_call(
        _moba_combine_body,
        grid=(B,),
        in_specs=[stat_spec, stat_spec, stat_spec,
                  pl.BlockSpec((None, n_pages, rows, dh), lambda b: (b, 0, 0, 0)),
                  q_spec, new_spec, new_spec],
        out_specs=q_spec,
        out_shape=jax.ShapeDtypeStruct((B, lq, width), F32),
        compiler_params=_params("parallel"),
        name="moba_combine",
    )(sc, m, l, o_pages, q, kn, vn)


def _pad_tokens(a, batch, length):
    a = a.reshape(batch, -1, a.shape[-1])
    return jnp.pad(a, ((0, 0), (0, length - a.shape[1]), (0, 0))).reshape(batch * length, a.shape[-1])


def _unpad_tokens(a, batch, seq):
    return a.reshape(batch, -1, a.shape[-1])[:, :seq].reshape(batch * seq, a.shape[-1])


def _row_tile(rows, target):
    n = max(1, round(rows / target))
    assert rows % n == 0 and (rows // n) % 8 == 0, rows
    return rows // n


class _Group(NamedTuple):
    batch: int
    seq: int
    row0: int
    hgrn_s0: jax.Array
    gla_s0: jax.Array
    past: Optional[tuple]


def _group_rows(proj, grp):
    if grp.seq % REC_CHUNK == 0:
        assert grp.row0 == 0
        return proj, grp.seq, REC_CHUNK
    own = proj[grp.row0:grp.row0 + grp.batch * grp.seq]
    seq = -(-grp.seq // REC_CHUNK) * REC_CHUNK
    assert seq == REC_CHUNK
    return _pad_tokens(own, grp.batch, seq), seq, grp.seq


def _even_mixer(proj, grp, e, lb, hgrn_norm_w):
    hw = HGRN_HEADS * HEAD_DIM
    mw = MOBA_HEADS * HEAD_DIM
    rows = grp.batch * grp.seq
    rec_in, seq, valid = _group_rows(proj[:, :4 * hw] if grp.seq % REC_CHUNK else proj, grp)
    o_h, s_h = hgrn_mix(rec_in, lb[e], hgrn_norm_w[e], grp.hgrn_s0[e], batch=grp.batch, seq=seq, valid_len=valid)
    o_h = _unpad_tokens(o_h, grp.batch, grp.seq)
    own = proj[grp.row0:grp.row0 + rows]
    k_new = own[:, 4 * hw + mw:4 * hw + 2 * mw]
    v_new = own[:, 4 * hw + 2 * mw:4 * hw + 3 * mw]
    if grp.past is None:
        assert grp.row0 == 0
        first = 4 * hw // HEAD_DIM
        o_m = moba_prompt(proj, batch=grp.batch, seq=grp.seq, q_col=first, k_col=first + MOBA_HEADS,
                          v_col=first + 2 * MOBA_HEADS)
    else:
        cache_k, cache_v, page_table = grp.past
        q = own[:, 4 * hw:4 * hw + mw].reshape(grp.batch, grp.seq, mw)
        sc, m, lsum, o_pages = moba_pages(q, cache_k, cache_v, page_table, e)
        heads = lambda a: a.reshape(grp.batch, grp.seq, MOBA_HEADS, HEAD_DIM)
        o_m = moba_combine(sc, m, lsum, o_pages, q, heads(k_new), heads(v_new)).reshape(rows, mw)
    return o_h, o_m, s_h, k_new, v_new


def _odd_mixer(proj, grp, o, gla_w_gate, gla_b_gate, gla_norm_w):
    rec_in, seq, valid = _group_rows(proj, grp)
    o_g, s_g = gla_mix(rec_in, gla_w_gate, o, gla_b_gate[o], gla_norm_w[o], grp.gla_s0[o], batch=grp.batch, seq=seq,
                       valid_len=valid)
    return _unpad_tokens(o_g, grp.batch, grp.seq), s_g


def _trunk(x, groups, lb, wts):
    (norm_w, ffn_gate, ffn_up, ffn_down, even_w_in, even_w_out, hgrn_norm_w, gla_w_in, gla_w_gate,
     gla_b_gate, gla_norm_w, gla_w_out, final_norm_w) = wts
    depth = norm_w.shape[0]
    tm = _row_tile(x.shape[0], 1024)
    out_tile = lambda grp: _row_tile(grp.batch * grp.seq, 512)
    ks, vs, hs, gs = ([[] for _ in groups] for _ in range(4))
    for l in range(depth):
        x = ffn(x, norm_w[l, 0], ffn_gate, ffn_up, ffn_down, l, 0, tm=tm)
        if l % 2 == 0:
            e = l // 2
            proj = norm_matmul(x, norm_w[l, 1], even_w_in, e, tm=tm, tn=1024)
            for gi, grp in enumerate(groups):
                o_h, o_m, s_h, k_new, v_new = _even_mixer(proj, grp, e, lb, hgrn_norm_w)
                x = out_proj(x, o_h, 0, o_m, 0, even_w_out, e, tm=out_tile(grp), row0=grp.row0)
                ks[gi].append(k_new)
                vs[gi].append(v_new)
                hs[gi].append(s_h)
        else:
            o = l // 2
            proj = norm_matmul(x, norm_w[l, 1], gla_w_in, o, tm=tm, tn=1280)
            for gi, grp in enumerate(groups):
                o_g, s_g = _odd_mixer(proj, grp, o, gla_w_gate, gla_b_gate, gla_norm_w)
                x = out_proj(x, o_g, 0, o_g, 1, gla_w_out, o, tm=out_tile(grp), row0=grp.row0)
                gs[gi].append(s_g)
        x = ffn(x, norm_w[l, 2], ffn_gate, ffn_up, ffn_down, l, 1, final_norm_w if l == depth - 1 else None, tm=tm)
    return x, ks, vs, hs, gs


def kernel(x_prompt, x_sample, cache_k, cache_v, state_hgrn, state_gla, page_table, norm_w, ffn_gate, ffn_up,
           ffn_down, even_w_in, even_w_out, hgrn_lb_logits, hgrn_norm_w, gla_w_in, gla_w_gate_up, gla_b_gate,
           gla_norm_w, gla_w_out, final_norm_w):
    lb = jnp.cumsum(jax.nn.softmax(hgrn_lb_logits.astype(F32), axis=0), axis=0)
    lb = lb - lb[0:1]

    n_odd, d_model, odd_in = gla_w_in.shape
    gate_col = odd_in - GLA_GATE_RANK
    assert gate_col % LANES == 0
    odd_cols = -(-(gate_col + LANES) // (5 * MXU_DIM)) * (5 * MXU_DIM)
    gla_w_in_p = jnp.pad(gla_w_in, ((0, 0), (0, 0), (0, odd_cols - odd_in))).astype(BF16)
    gla_w_gate_p = jnp.pad(gla_w_gate_up, ((0, 0), (0, LANES - GLA_GATE_RANK), (0, 0))).astype(BF16)
    wts = (norm_w, ffn_gate, ffn_up, ffn_down.astype(BF16), even_w_in.astype(BF16), even_w_out.astype(BF16), hgrn_norm_w,
           gla_w_in_p, gla_w_gate_p, gla_b_gate, gla_norm_w, gla_w_out.astype(BF16), final_norm_w)

    Bp, Lp, D = x_prompt.shape
    Bs, Ls, _ = x_sample.shape
    n_even = state_hgrn.shape[0]
    prompt = _Group(Bp, Lp, 0, jnp.zeros((n_even, Bp) + state_hgrn.shape[2:], F32),
                    jnp.zeros((n_odd, Bp) + state_gla.shape[2:], F32), None)
    sample = _Group(Bs, Ls, Bp * Lp, state_hgrn, state_gla, (cache_k, cache_v, page_table))
    total = Bp * Lp + Bs * Ls
    x = jnp.pad(x_prompt.reshape(Bp * Lp, D), ((0, Bs * Ls + -total % LANES), (0, 0)))
    x = lax.dynamic_update_slice(x, x_sample.reshape(Bs * Ls, D), (Bp * Lp, 0))
    y, ks, vs, hs, gs = _trunk(x, (prompt, sample), lb, wts)

    pages = lambda a: a.reshape(Bp, Lp // PAGE_SIZE, PAGE_SIZE, MOBA_HEADS, HEAD_DIM)
    rows = lambda a: a.reshape(Bs, Ls, MOBA_HEADS, HEAD_DIM)
    return (y[:Bp * Lp].reshape(Bp, Lp, D), y[Bp * Lp:total].reshape(Bs, Ls, D),
            jnp.stack([pages(a) for a in ks[0]]), jnp.stack([pages(a) for a in vs[0]]),
            jnp.stack(hs[0]), jnp.stack(gs[0]),
            jnp.stack([rows(a) for a in ks[1]]), jnp.stack([rows(a) for a in vs[1]]),
            jnp.stack(hs[1]), jnp.stack(gs[1]))
```

```python
import functools
from typing import NamedTuple, Optional

import numpy as np
import jax
import jax.numpy as jnp
from jax import lax
from jax.experimental import pallas as pl
from jax.experimental.pallas import tpu as pltpu

F32 = jnp.float32
BF16 = jnp.bfloat16

NORM_EPS = 1e-6
HEAD_DIM = 128
HGRN_HEADS = 8
MOBA_HEADS = 8
MOBA_BLOCK = 256
MOBA_TOPK = 3
PAGE_SIZE = 128
GLA_HEADS = 4
GLA_GATE_RANK = 16
GLA_GATE_NORM = 16.0

LANES = 128
MXU_DIM = 256
REC_CHUNK = 128
REC_LEVELS = 7
REC_SMALL_HALVES = (4, 2)
MOBA_KEY_TILE = 4
MOBA_PAGE_GROUP = 8
MASKED = -1e30
VMEM_LIMIT = 60 * 1024 * 1024

NT_DIMS = (((1,), (1,)), ((), ()))
TN_DIMS = (((0,), (0,)), ((), ()))
NEG_INF = float("-inf")


def _params(*sem):
    return pltpu.CompilerParams(dimension_semantics=sem, vmem_limit_bytes=VMEM_LIMIT)


def _rms(x, w):
    return x * lax.rsqrt(jnp.mean(x * x, axis=-1, keepdims=True) + NORM_EPS) * w


def _silu(x):
    return x * jax.nn.sigmoid(x)


def _bdot(a, b):
    return jnp.dot(a.astype(BF16), b.astype(BF16), preferred_element_type=F32)


def _bdot_nt(a, b):
    return lax.dot_general(a.astype(BF16), b.astype(BF16), NT_DIMS, preferred_element_type=F32)


def _ffn_body(x_ref, nw_ref, wg_ref, wu_ref, wd_ref, *rest, final):
    if final:
        fw_ref, o_ref, h_ref, a_ref = rest
    else:
        o_ref, h_ref, a_ref = rest
    j = pl.program_id(1)
    last = pl.num_programs(1) - 1

    def hidden():
        h = h_ref[...]
        g = jnp.dot(h, wg_ref[...].astype(BF16), preferred_element_type=F32)
        u = jnp.dot(h, wu_ref[...].astype(BF16), preferred_element_type=F32)
        return (_silu(g) * u).astype(BF16)

    def down():
        return jnp.dot(a_ref[...], wd_ref[...].astype(BF16), preferred_element_type=F32)

    @pl.when(j == 0)
    def _():
        h_ref[...] = _rms(x_ref[...], nw_ref[...]).astype(BF16)
        o_ref[...] = jnp.zeros_like(o_ref)
        a_ref[...] = hidden()

    @pl.when((j > 0) & (j < last))
    def _():
        d = down()
        a_ref[...] = hidden()
        o_ref[...] += d

    @pl.when(j == last)
    def _():
        y = x_ref[...] + 0.5 * (o_ref[...] + down())
        if final:
            y = _rms(y, fw_ref[...])
        o_ref[...] = y


def ffn(x, nw, wg, wu, wd, layer, which, final_w=None, *, tm, tf=256):
    M, D = x.shape
    FF = wg.shape[-1]
    assert M % tm == 0 and FF % tf == 0
    nf = FF // tf
    final = final_w is not None
    up_tile = lambda i, j: (layer, which, 0, jnp.minimum(j, nf - 1))
    in_specs = [
        pl.BlockSpec((tm, D), lambda i, j: (i, 0)),
        pl.BlockSpec((1, D), lambda i, j: (0, 0)),
        pl.BlockSpec((None, None, D, tf), up_tile),
        pl.BlockSpec((None, None, D, tf), up_tile),
        pl.BlockSpec((None, None, tf, D), lambda i, j: (layer, which, jnp.maximum(j - 1, 0), 0)),
    ]
    args = [x, nw.reshape(1, D), wg, wu, wd]
    if final:
        in_specs.append(pl.BlockSpec((1, D), lambda i, j: (0, 0)))
        args.append(final_w.reshape(1, D))
    return pl.pallas_call(
        functools.partial(_ffn_body, final=final),
        grid=(M // tm, nf + 1),
        in_specs=in_specs,
        out_specs=pl.BlockSpec((tm, D), lambda i, j: (i, 0)),
        out_shape=jax.ShapeDtypeStruct((M, D), F32),
        scratch_shapes=[pltpu.VMEM((tm, D), BF16), pltpu.VMEM((tm, tf), BF16)],
        compiler_params=_params("parallel", "arbitrary"),
        name="ffn",
    )(*args)


def _norm_mm_body(x_ref, nw_ref, w_ref, o_ref, h_ref):
    @pl.when(pl.program_id(1) == 0)
    def _():
        h_ref[...] = _rms(x_ref[...], nw_ref[...]).astype(BF16)

    o_ref[...] = jnp.dot(h_ref[...], w_ref[...], preferred_element_type=F32)


def norm_matmul(x, nw, w, layer, *, tm, tn):
    M, D = x.shape
    N = w.shape[-1]
    assert M % tm == 0 and N % tn == 0
    return pl.pallas_call(
        _norm_mm_body,
        grid=(M // tm, N // tn),
        in_specs=[
            pl.BlockSpec((tm, D), lambda i, j: (i, 0)),
            pl.BlockSpec((1, D), lambda i, j: (0, 0)),
            pl.BlockSpec((None, D, tn), lambda i, j: (layer, 0, j)),
        ],
        out_specs=pl.BlockSpec((tm, tn), lambda i, j: (i, j)),
        out_shape=jax.ShapeDtypeStruct((M, N), F32),
        scratch_shapes=[pltpu.VMEM((tm, D), BF16)],
        compiler_params=_params("parallel", "arbitrary"),
        name="norm_matmul",
    )(x, nw.reshape(1, D), w)


def _out_proj_body(x_ref, a_ref, b_ref, wa_ref, wb_ref, o_ref):
    o_ref[...] = x_ref[...] + _bdot(a_ref[...], wa_ref[...]) + _bdot(b_ref[...], wb_ref[...])


def out_proj(x, a, a_blk, b, b_blk, w, layer, *, tm, row0):
    D = x.shape[1]
    rows = a.shape[0]
    K = w.shape[1] // 2
    assert rows % tm == 0 and row0 % tm == 0
    first = row0 // tm
    return pl.pallas_call(
        _out_proj_body,
        grid=(rows // tm,),
        in_specs=[
            pl.BlockSpec((tm, D), lambda i: (first + i, 0)),
            pl.BlockSpec((tm, K), lambda i: (i, a_blk)),
            pl.BlockSpec((tm, K), lambda i: (i, b_blk)),
            pl.BlockSpec((None, K, D), lambda i: (layer, 0, 0)),
            pl.BlockSpec((None, K, D), lambda i: (layer, 1, 0)),
        ],
        out_specs=pl.BlockSpec((tm, D), lambda i: (first + i, 0)),
        out_shape=jax.ShapeDtypeStruct(x.shape, F32),
        input_output_aliases={0: 0},
        compiler_params=_params("parallel"),
        name="out_proj",
    )(x, a, b, w, w)


def _rec_constants():
    C = REC_CHUNK
    t = np.arange(C)[:, None]
    s = np.arange(C)[None, :]
    mats = [(s <= t).astype(np.float32)]
    a_mats, b_mats = [], []
    level = np.full((C, C), -1, np.int32)
    level[np.arange(C), np.arange(C)] = REC_LEVELS
    for li in range(REC_LEVELS):
        L = C >> (li + 1)
        mid = (t // (2 * L)) * (2 * L) + L
        if L in REC_SMALL_HALVES:
            a_mats.append(((t >= mid) & (s >= mid) & (s <= t)).astype(np.float32))
            b_mats.append(((t < mid) & (s >= t + 1) & (s <= mid - 1)).astype(np.float32))
        mid_s = (s // (2 * L)) * (2 * L) + L
        level[(t >= mid) & (s < mid_s) & (mid_s == mid)] = li
    w = np.concatenate(mats + a_mats + b_mats, axis=0)
    return jnp.asarray(w, BF16), jnp.asarray(level)


def _level_exponents(li, b, g, e):
    C = REC_CHUNK
    L = C >> (li + 1)
    if L == 1:
        return g, None
    if L in REC_SMALL_HALVES:
        i = REC_SMALL_HALVES.index(L)
        n = len(REC_SMALL_HALVES)
        return e[(1 + i) * C:(2 + i) * C], e[(1 + n + i) * C:(2 + n + i) * C]
    c = jnp.concatenate([jnp.broadcast_to(b[m + L - 1:m + L, :], (2 * L, b.shape[1])) for m in range(0, C, 2 * L)],
                        axis=0)
    d = b - c
    return jnp.minimum(d, 0.0), jnp.minimum(-d, 0.0)


def _rec_core(q, k, vs, g, w_ref, level, st_ref):
    C = REC_CHUNK
    heads = len(vs)
    dk = q.shape[1] // heads
    head = lambda a, h: a[:, h * dk:(h + 1) * dk]
    g1 = g.astype(BF16)
    r1 = g - g1.astype(F32)
    g2 = r1.astype(BF16)
    g3 = (r1 - g2.astype(F32)).astype(BF16)
    w = w_ref[...]
    e = (jnp.dot(w, g1, preferred_element_type=F32) + jnp.dot(w, g2, preferred_element_type=F32)
         + jnp.dot(w, g3, preferred_element_type=F32))
    b = e[0:C]
    b_last = b[C - 1:C, :]
    sts = [st_ref[h] for h in range(heads)]
    qb = (q * jnp.exp(b)).astype(BF16)
    inters = [lax.dot_general(head(qb, h), sts[h].astype(BF16), NT_DIMS, preferred_element_type=F32)
              for h in range(heads)]
    atts = [jnp.where(level == REC_LEVELS, _bdot_nt(head(q, h), head(k, h)), 0.0) for h in range(heads)]
    for li in range(REC_LEVELS):
        eq, ek = _level_exponents(li, b, g, e)
        ql = (q * jnp.exp(eq)).astype(BF16)
        kl = (k if ek is None else k * jnp.exp(ek)).astype(BF16)
        for h in range(heads):
            part = lax.dot_general(head(ql, h), head(kl, h), NT_DIMS, preferred_element_type=F32)
            atts[h] = atts[h] + jnp.where(level == li, part, 0.0)
    kd = (k * jnp.exp(b_last - b)).astype(BF16)
    decay = jnp.exp(b_last)
    outs = []
    for h in range(heads):
        outs.append(inters[h] + _bdot(atts[h], vs[h]))
        upd = lax.dot_general(vs[h].astype(BF16), head(kd, h), TN_DIMS, preferred_element_type=F32)
        st_ref[h] = sts[h] * head(decay, h) + upd
    return outs


def _valid_rows(shape, valid_len):
    return lax.broadcasted_iota(jnp.int32, shape, 0) < valid_len


def _load_state(s0_ref, st_ref):
    @pl.when(pl.program_id(2) == 0)
    def _():
        for hh in range(st_ref.shape[0]):
            st_ref[hh] = s0_ref[hh].T


def _store_state(so_ref, st_ref):
    @pl.when(pl.program_id(2) == pl.num_programs(2) - 1)
    def _():
        for hh in range(st_ref.shape[0]):
            so_ref[hh] = st_ref[hh].T


def _hgrn_body(hq_ref, hf_ref, hi_ref, hg_ref, lb_ref, nw_ref, s0_ref, w_ref, lvl_ref, o_ref, so_ref, st_ref,
               *, valid_len):
    _load_state(s0_ref, st_ref)
    heads, dv, dk = st_ref.shape
    lb = lb_ref[...]
    f = lb + (1.0 - lb) * jax.nn.sigmoid(hf_ref[...])
    k = 1.0 - f
    g = jnp.log(f)
    if valid_len < REC_CHUNK:
        ok = _valid_rows(g.shape, valid_len)
        k = jnp.where(ok, k, 0.0)
        g = jnp.where(ok, g, 0.0)
    q = _silu(hq_ref[...]) * (dk ** -0.5)
    hs = range(heads)
    outs = _rec_core(q, k, [hi_ref[:, h * dv:(h + 1) * dv] for h in hs], g, w_ref, lvl_ref[...], st_ref)
    for h in hs:
        o_ref[:, h * dv:(h + 1) * dv] = _rms(outs[h], nw_ref[...]) * _silu(hg_ref[:, h * dv:(h + 1) * dv])
    _store_state(so_ref, st_ref)


def hgrn_mix(proj, lb, norm_w, s0, *, batch, seq, valid_len=REC_CHUNK, heads_per_step=8):
    H, dk, dv = s0.shape[1:]
    C = REC_CHUNK
    hb = heads_per_step
    assert H % hb == 0
    n = seq // C
    wmat, level = _rec_constants()
    row = lambda b, h, c: b * n + c
    col_spec = lambda grp: pl.BlockSpec((C, hb * dk), lambda b, h, c: (row(b, h, c), grp * (H // hb) + h))
    const = lambda shape: pl.BlockSpec(shape, lambda b, h, c: (0,) * len(shape))
    state_spec = pl.BlockSpec((None, hb, dk, dv), lambda b, h, c: (b, h, 0, 0))
    return pl.pallas_call(
        functools.partial(_hgrn_body, valid_len=valid_len),
        grid=(batch, H // hb, n),
        in_specs=[col_spec(0), col_spec(1), col_spec(2), col_spec(3),
                  pl.BlockSpec((1, hb * dk), lambda b, h, c: (0, h)),
                  const((1, dv)), state_spec, const(wmat.shape), const(level.shape)],
        out_specs=[pl.BlockSpec((C, hb * dv), lambda b, h, c: (row(b, h, c), h)), state_spec],
        out_shape=[jax.ShapeDtypeStruct((batch * n * C, H * dv), F32), jax.ShapeDtypeStruct(s0.shape, F32)],
        scratch_shapes=[pltpu.VMEM((hb, dv, dk), F32)],
        compiler_params=_params("parallel", "parallel", "arbitrary"),
        name="hgrn_mix",
    )(proj, proj, proj, proj, lb.reshape(1, H * dk), norm_w.reshape(1, dv), s0, wmat, level)


def _gla_body(q_ref, k_ref, v_ref, r_ref, a_ref, wg_ref, bg_ref, nw_ref, s0_ref, w_ref, lvl_ref, o_ref, so_ref,
              st_ref, *, valid_len):
    _load_state(s0_ref, st_ref)
    heads, dv, dk = st_ref.shape
    x = _bdot(a_ref[...], wg_ref[...]) + bg_ref[...]
    g = -(jnp.maximum(-x, 0.0) + jnp.log(1.0 + jnp.exp(-jnp.abs(x)))) / GLA_GATE_NORM
    k = k_ref[...]
    if valid_len < REC_CHUNK:
        ok = _valid_rows(g.shape, valid_len)
        k = jnp.where(ok, k, 0.0)
        g = jnp.where(ok, g, 0.0)
    q = q_ref[...] * (dk ** -0.5)
    hs = range(heads)
    outs = _rec_core(q, k, [v_ref[:, h * dv:(h + 1) * dv] for h in hs], g, w_ref, lvl_ref[...], st_ref)
    for h in hs:
        o_ref[:, h * dv:(h + 1) * dv] = _rms(outs[h], nw_ref[...]) * _silu(r_ref[:, h * dv:(h + 1) * dv])
    _store_state(so_ref, st_ref)


def gla_mix(proj, w_gate, layer, b_gate, norm_w, s0, *, batch, seq, valid_len=REC_CHUNK, heads_per_step=4):
    H, dk, dv = s0.shape[1:]
    C = REC_CHUNK
    hb = heads_per_step
    assert H % hb == 0
    n = seq // C
    nh = H // hb
    wmat, level = _rec_constants()
    row = lambda b, h, c: b * n + c
    const = lambda shape: pl.BlockSpec(shape, lambda b, h, c: (0,) * len(shape))
    state_spec = pl.BlockSpec((None, hb, dk, dv), lambda b, h, c: (b, h, 0, 0))
    kv_off = 2 * H * dk // (hb * dv)
    a_blk = (2 * H * dk + 2 * H * dv) // LANES
    return pl.pallas_call(
        functools.partial(_gla_body, valid_len=valid_len),
        grid=(batch, nh, n),
        in_specs=[pl.BlockSpec((C, hb * dk), lambda b, h, c: (row(b, h, c), h)),
                  pl.BlockSpec((C, hb * dk), lambda b, h, c: (row(b, h, c), nh + h)),
                  pl.BlockSpec((C, hb * dv), lambda b, h, c: (row(b, h, c), kv_off + h)),
                  pl.BlockSpec((C, hb * dv), lambda b, h, c: (row(b, h, c), kv_off + nh + h)),
                  pl.BlockSpec((C, LANES), lambda b, h, c: (row(b, h, c), a_blk)),
                  pl.BlockSpec((None, LANES, hb * dk), lambda b, h, c: (layer, 0, h)),
                  pl.BlockSpec((1, hb * dk), lambda b, h, c: (0, h)),
                  const((1, dv)), state_spec, const(wmat.shape), const(level.shape)],
        out_specs=[pl.BlockSpec((C, hb * dv), lambda b, h, c: (row(b, h, c), h)), state_spec],
        out_shape=[jax.ShapeDtypeStruct((batch * n * C, H * dv), F32), jax.ShapeDtypeStruct(s0.shape, F32)],
        scratch_shapes=[pltpu.VMEM((hb, dv, dk), F32)],
        compiler_params=_params("parallel", "parallel", "arbitrary"),
        name="gla_mix",
    )(proj, proj, proj, proj, proj, w_gate, b_gate.reshape(1, H * dk), norm_w.reshape(1, dv), s0, wmat, level)


def _top_blocks(s, index, axis):
    sel = jnp.zeros(s.shape, F32)
    for _ in range(MOBA_TOPK):
        m = jnp.max(s, axis=axis, keepdims=True)
        idx = jnp.min(jnp.where(s == m, index, s.shape[axis]), axis=axis, keepdims=True)
        hit = index == idx
        sel = jnp.where(hit & (m > NEG_INF), 1.0, sel)
        s = jnp.where(hit, NEG_INF, s)
    return sel


def _moba_prompt_body(q_ref, k_ref, v_ref, o_ref, kmean_ref, kaug_ref, vt_ref, lg_ref):
    i = pl.program_id(2)
    blk = MOBA_BLOCK
    tile = MOBA_KEY_TILE * blk
    T = k_ref.shape[0]
    nb = T // blk
    heads, nbp, dh = kmean_ref.shape
    hs = range(heads)
    cols = lambda h: slice(h * dh, (h + 1) * dh)

    @pl.when(i == 0)
    def _():
        kmean_ref[...] = jnp.zeros_like(kmean_ref)
        key_blk = lax.broadcasted_iota(jnp.int32, (T, LANES), 0) // blk
        on_blk = lax.broadcasted_iota(jnp.int32, (T, LANES), 1) == key_blk
        for h in hs:
            for jb in range(nb):
                rows = slice(jb * blk, (jb + 1) * blk)
                kmean_ref[h, jb:jb + 1, :] = jnp.mean(k_ref[rows, cols(h)], axis=0, keepdims=True)
                vt_ref[h, :, rows] = v_ref[rows, cols(h)].T.astype(BF16)
            kaug_ref[h, :, 0:dh] = k_ref[:, cols(h)].astype(BF16)
            kaug_ref[h, :, dh:dh + LANES] = jnp.where(on_blk, MASKED, 0.0).astype(BF16)

    own = pl.multiple_of(i * blk, blk)
    blk_id = lax.broadcasted_iota(jnp.int32, (nbp, blk), 0)
    visible = lax.broadcasted_iota(jnp.int32, (blk, blk), 0) <= lax.broadcasted_iota(jnp.int32, (blk, blk), 1)
    qaugs, m0 = [], []
    for h in hs:
        q = q_ref[:, cols(h)]
        s = lax.dot_general(kmean_ref[h], q, NT_DIMS, precision=lax.Precision.HIGHEST, preferred_element_type=F32)
        sel = _top_blocks(jnp.where(blk_id < i, s, NEG_INF), blk_id, 0)
        qst = (q * (dh ** -0.5)).T.astype(BF16)
        unpicked = jnp.concatenate([1.0 - sel, jnp.zeros((LANES - nbp, blk), F32)], axis=0).astype(BF16)
        qaugs.append(jnp.concatenate([qst, unpicked], axis=0))
        lg_own = jnp.dot(kaug_ref[h, pl.ds(own, blk), 0:dh], qst, preferred_element_type=F32)
        lg_own = jnp.where(visible, lg_own, NEG_INF)
        lg_ref[h, T:T + blk, :] = lg_own
        m0.append(jnp.max(lg_own, axis=0, keepdims=True))
    n_tiles = (i + MOBA_KEY_TILE - 1) // MOBA_KEY_TILE

    def score(t, ms):
        start = pl.multiple_of(t * tile, tile)
        out = []
        for h in hs:
            lg = jnp.dot(kaug_ref[h, pl.ds(start, tile), :], qaugs[h], preferred_element_type=F32)
            lg_ref[h, pl.ds(start, tile), :] = lg
            out.append(jnp.maximum(ms[h], jnp.max(lg, axis=0, keepdims=True)))
        return tuple(out)

    n_pairs = n_tiles // 2
    pair = lambda body: (lambda tp, carry: body(2 * tp + 1, body(2 * tp, carry)))
    ms = lax.fori_loop(0, n_pairs, pair(score), tuple(m0))
    ms = lax.fori_loop(2 * n_pairs, n_tiles, score, ms)

    init = []
    for h in hs:
        p_own = jnp.exp(lg_ref[h, T:T + blk, :] - ms[h])
        acc0 = jnp.dot(vt_ref[h, :, pl.ds(own, blk)], p_own.astype(BF16), preferred_element_type=F32)
        init.append((jnp.sum(p_own, axis=0, keepdims=True), acc0))

    def gather(t, carry):
        start = pl.multiple_of(t * tile, tile)
        out = []
        for h in hs:
            l, acc = carry[h]
            p = jnp.exp(lg_ref[h, pl.ds(start, tile), :] - ms[h])
            acc = acc + jnp.dot(vt_ref[h, :, pl.ds(start, tile)], p.astype(BF16), preferred_element_type=F32)
            out.append((l + jnp.sum(p, axis=0, keepdims=True), acc))
        return tuple(out)

    carry = lax.fori_loop(0, n_pairs, pair(gather), tuple(init))
    carry = lax.fori_loop(2 * n_pairs, n_tiles, gather, carry)
    for h in hs:
        l, acc = carry[h]
        o_ref[:, cols(h)] = (acc / l).T


def moba_prompt(proj, *, batch, seq, q_col, k_col, v_col, heads_per_step=2):
    H, dh, blk = MOBA_HEADS, HEAD_DIM, MOBA_BLOCK
    T = seq
    hb = heads_per_step
    assert T % (MOBA_KEY_TILE * blk) == 0 and T // blk <= LANES
    assert H % hb == 0 and q_col % hb == 0 and k_col % hb == 0 and v_col % hb == 0
    nq = T // blk
    nbp = -(-nq // 8) * 8
    return pl.pallas_call(
        _moba_prompt_body,
        grid=(batch, H // hb, nq),
        in_specs=[pl.BlockSpec((blk, hb * dh), lambda b, h, i: (b * nq + i, q_col // hb + h)),
                  pl.BlockSpec((T, hb * dh), lambda b, h, i: (b, k_col // hb + h)),
                  pl.BlockSpec((T, hb * dh), lambda b, h, i: (b, v_col // hb + h))],
        out_specs=pl.BlockSpec((blk, hb * dh), lambda b, h, i: (b * nq + i, h)),
        out_shape=jax.ShapeDtypeStruct((batch * T, H * dh), F32),
        scratch_shapes=[pltpu.VMEM((hb, nbp, dh), F32), pltpu.VMEM((hb, T, dh + LANES), BF16),
                        pltpu.VMEM((hb, dh, T), BF16), pltpu.VMEM((hb, T + blk, blk), F32)],
        compiler_params=_params("parallel", "parallel", "arbitrary"),
        name="moba_prompt",
    )(proj, proj, proj)


def _stack_heads(q):
    heads = q.shape[1] // HEAD_DIM
    return jnp.concatenate([q[:, h * HEAD_DIM:(h + 1) * HEAD_DIM] for h in range(heads)], axis=0)


def _same_head(rows, cols, lq, heads):
    r = lax.broadcasted_iota(jnp.int32, (rows, cols), 0)
    c = lax.broadcasted_iota(jnp.int32, (rows, cols), 1)
    return (r // lq) == (c % heads)


def _moba_pages_body(pt_ref, q_ref, *refs):
    group = MOBA_PAGE_GROUP
    k_refs, v_refs = refs[:group], refs[group:2 * group]
    sc_ref, m_ref, l_ref, o_ref = refs[2 * group:]
    step = pl.program_id(1)
    lq = q_ref.shape[0]
    n_tok, heads, dh = k_refs[0].shape

    @pl.when(step == 0)
    def _():
        sc_ref[...] = jnp.zeros_like(sc_ref)
        m_ref[...] = jnp.zeros_like(m_ref)
        l_ref[...] = jnp.zeros_like(l_ref)

    qa = _stack_heads(q_ref[...])
    rows = qa.shape[0]
    qs = (qa * (dh ** -0.5)).astype(BF16)
    same = _same_head(rows, n_tok * heads, lq, heads)
    lane = lax.broadcasted_iota(jnp.int32, sc_ref.shape, 1)
    sc_all, m_all, l_all = sc_ref[...], m_ref[...], l_ref[...]
    pages = range(group)
    logits = [lax.dot_general(qs, k_refs[gi][...].reshape(n_tok * heads, dh).astype(BF16), NT_DIMS,
                              preferred_element_type=F32) for gi in pages]
    logits = [jnp.where(same, lg, NEG_INF) for lg in logits]
    ms = [jnp.max(lg, axis=1, keepdims=True) for lg in logits]
    prs = [jnp.exp(logits[gi] - ms[gi]) for gi in pages]
    for gi in pages:
        o_ref[gi] = _bdot(prs[gi], v_refs[gi][...].reshape(n_tok * heads, dh))
    for gi in pages:
        ksum = jnp.sum(k_refs[gi][...], axis=0)
        ksum_rows = jnp.concatenate([jnp.broadcast_to(ksum[h:h + 1, :], (lq, dh)) for h in range(heads)], axis=0)
        here = lane == step * group + gi
        sc_all = jnp.where(here, jnp.sum(qa * ksum_rows, axis=1, keepdims=True), sc_all)
        m_all = jnp.where(here, ms[gi], m_all)
        l_all = jnp.where(here, jnp.sum(prs[gi], axis=1, keepdims=True), l_all)
    sc_ref[...] = sc_all
    m_ref[...] = m_all
    l_ref[...] = l_all


def moba_pages(q, cache_k, cache_v, page_table, layer):
    B, lq, width = q.shape
    _, _, n_tok, heads, dh = cache_k.shape
    n_pages = page_table.shape[1]
    group = MOBA_PAGE_GROUP
    assert n_pages == LANES and heads * dh == width and n_pages % group == 0
    rows = heads * lq
    page_spec = lambda gi: pl.BlockSpec((None, None, n_tok, heads, dh),
                                        lambda b, p, pt: (layer, pt[b, p * group + gi], 0, 0, 0))
    stat_spec = pl.BlockSpec((None, rows, LANES), lambda b, p, pt: (b, 0, 0))
    stat = jax.ShapeDtypeStruct((B, rows, LANES), F32)
    page_specs = [page_spec(gi) for gi in range(group)]
    return pl.pallas_call(
        _moba_pages_body,
        grid_spec=pltpu.PrefetchScalarGridSpec(
            num_scalar_prefetch=1,
            grid=(B, n_pages // group),
            in_specs=[pl.BlockSpec((None, lq, width), lambda b, p, pt: (b, 0, 0))] + page_specs + page_specs,
            out_specs=[stat_spec, stat_spec, stat_spec,
                       pl.BlockSpec((None, group, rows, dh), lambda b, p, pt: (b, p, 0, 0))],
        ),
        out_shape=[stat, stat, stat, jax.ShapeDtypeStruct((B, n_pages, rows, dh), F32)],
        compiler_params=_params("parallel", "arbitrary"),
        name="moba_pages",
    )(page_table, q, *([cache_k] * group), *([cache_v] * group))


def _moba_combine_body(sc_ref, m_ref, l_ref, op_ref, q_ref, kn_ref, vn_ref, o_ref):
    lq = q_ref.shape[0]
    rows = sc_ref.shape[0]
    heads = rows // lq
    n_pages = op_ref.shape[0]
    pages_per_block = MOBA_BLOCK // PAGE_SIZE
    assert pages_per_block == 2
    lane = lax.broadcasted_iota(jnp.int32, (rows, LANES), 1)
    sc = sc_ref[...]
    bs = (sc + pltpu.roll(sc, LANES - 1, 1)) * (1.0 / MOBA_BLOCK)
    sel = _top_blocks(jnp.where(lane % pages_per_block == 0, bs, NEG_INF), lane, 1)
    sel = sel + pltpu.roll(sel, 1, 1)

    qa = _stack_heads(q_ref[...])
    lo = _bdot_nt(qa * (HEAD_DIM ** -0.5), kn_ref[...])
    t_q = lax.broadcasted_iota(jnp.int32, (rows, LANES), 0) % lq
    lo = jnp.where(_same_head(rows, LANES, lq, heads) & (lane // heads <= t_q), lo, NEG_INF)
    m = m_ref[...]
    m_all = jnp.maximum(jnp.max(lo, axis=1, keepdims=True),
                        jnp.max(jnp.where(sel > 0.0, m, NEG_INF), axis=1, keepdims=True))
    w = jnp.where(sel > 0.0, jnp.exp(m - m_all), 0.0)
    p_own = jnp.exp(lo - m_all)
    den = jnp.sum(w * l_ref[...], axis=1, keepdims=True) + jnp.sum(p_own, axis=1, keepdims=True)
    num0 = _bdot(p_own, vn_ref[...])

    num = num0
    for p in range(n_pages):
        num = num + w[:, p:p + 1] * op_ref[p]
    out = num / den
    o_ref[...] = jnp.concatenate([out[h * lq:(h + 1) * lq, :] for h in range(heads)], axis=1)


def moba_combine(sc, m, l, o_pages, q, k_new, v_new):
    B, lq, width = q.shape
    heads, dh = k_new.shape[2:]
    rows = sc.shape[1]
    n_pages = o_pages.shape[1]
    assert LANES % heads == 0 and lq * heads <= LANES
    pad = ((0, 0), (0, LANES // heads - lq), (0, 0), (0, 0))
    kn = jnp.pad(k_new, pad).reshape(B, LANES, dh)
    vn = jnp.pad(v_new, pad).reshape(B, LANES, dh)
    stat_spec = pl.BlockSpec((None, rows, LANES), lambda b: (b, 0, 0))
    new_spec = pl.BlockSpec((None, LANES, dh), lambda b: (b, 0, 0))
    q_spec = pl.BlockSpec((None, lq, width), lambda b: (b, 0, 0))
    return pl.pallas_call(
        _moba_combine_body,
        grid=(B,),
        in_specs=[stat_spec, stat_spec, stat_spec,
                  pl.BlockSpec((None, n_pages, rows, dh), lambda b: (b, 0, 0, 0)),
                  q_spec, new_spec, new_spec],
        out_specs=q_spec,
        out_shape=jax.ShapeDtypeStruct((B, lq, width), F32),
        compiler_params=_params("parallel"),
        name="moba_combine",
    )(sc, m, l, o_pages, q, kn, vn)


def _pad_tokens(a, batch, length):
    a = a.reshape(batch, -1, a.shape[-1])
    return jnp.pad(a, ((0, 0), (0, length - a.shape[1]), (0, 0))).reshape(batch * length, a.shape[-1])


def _unpad_tokens(a, batch, seq):
    return a.reshape(batch, -1, a.shape[-1])[:, :seq].reshape(batch * seq, a.shape[-1])


def _row_tile(rows, target):
    n = max(1, round(rows / target))
    assert rows % n == 0 and (rows // n) % 8 == 0, rows
    return rows // n


class _Group(NamedTuple):
    batch: int
    seq: int
    row0: int
    hgrn_s0: jax.Array
    gla_s0: jax.Array
    past: Optional[tuple]


def _group_rows(proj, grp):
    if grp.seq % REC_CHUNK == 0:
        assert grp.row0 == 0
        return proj, grp.seq, REC_CHUNK
    own = proj[grp.row0:grp.row0 + grp.batch * grp.seq]
    seq = -(-grp.seq // REC_CHUNK) * REC_CHUNK
    assert seq == REC_CHUNK
    return _pad_tokens(own, grp.batch, seq), seq, grp.seq


def _even_mixer(proj, grp, e, lb, hgrn_norm_w):
    hw = HGRN_HEADS * HEAD_DIM
    mw = MOBA_HEADS * HEAD_DIM
    rows = grp.batch * grp.seq
    rec_in, seq, valid = _group_rows(proj[:, :4 * hw] if grp.seq % REC_CHUNK else proj, grp)
    o_h, s_h = hgrn_mix(rec_in, lb[e], hgrn_norm_w[e], grp.hgrn_s0[e], batch=grp.batch, seq=seq, valid_len=valid)
    o_h = _unpad_tokens(o_h, grp.batch, grp.seq)
    own = proj[grp.row0:grp.row0 + rows]
    k_new = own[:, 4 * hw + mw:4 * hw + 2 * mw]
    v_new = own[:, 4 * hw + 2 * mw:4 * hw + 3 * mw]
    if grp.past is None:
        assert grp.row0 == 0
        first = 4 * hw // HEAD_DIM
        o_m = moba_prompt(proj, batch=grp.batch, seq=grp.seq, q_col=first, k_col=first + MOBA_HEADS,
                          v_col=first + 2 * MOBA_HEADS)
    else:
        cache_k, cache_v, page_table = grp.past
        q = own[:, 4 * hw:4 * hw + mw].reshape(grp.batch, grp.seq, mw)
        sc, m, lsum, o_pages = moba_pages(q, cache_k, cache_v, page_table, e)
        heads = lambda a: a.reshape(grp.batch, grp.seq, MOBA_HEADS, HEAD_DIM)
        o_m = moba_combine(sc, m, lsum, o_pages, q, heads(k_new), heads(v_new)).reshape(rows, mw)
    return o_h, o_m, s_h, k_new, v_new


def _odd_mixer(proj, grp, o, gla_w_gate, gla_b_gate, gla_norm_w):
    rec_in, seq, valid = _group_rows(proj, grp)
    o_g, s_g = gla_mix(rec_in, gla_w_gate, o, gla_b_gate[o], gla_norm_w[o], grp.gla_s0[o], batch=grp.batch, seq=seq,
                       valid_len=valid)
    return _unpad_tokens(o_g, grp.batch, grp.seq), s_g


def _trunk(x, groups, lb, wts):
    (norm_w, ffn_gate, ffn_up, ffn_down, even_w_in, even_w_out, hgrn_norm_w, gla_w_in, gla_w_gate,
     gla_b_gate, gla_norm_w, gla_w_out, final_norm_w) = wts
    depth = norm_w.shape[0]
    tm = _row_tile(x.shape[0], 1024)
    out_tile = lambda grp: _row_tile(grp.batch * grp.seq, 512)
    ks, vs, hs, gs = ([[] for _ in groups] for _ in range(4))
    for l in range(depth):
        x = ffn(x, norm_w[l, 0], ffn_gate, ffn_up, ffn_down, l, 0, tm=tm)
        if l % 2 == 0:
            e = l // 2
            proj = norm_matmul(x, norm_w[l, 1], even_w_in, e, tm=tm, tn=1024)
            for gi, grp in enumerate(groups):
                o_h, o_m, s_h, k_new, v_new = _even_mixer(proj, grp, e, lb, hgrn_norm_w)
                x = out_proj(x, o_h, 0, o_m, 0, even_w_out, e, tm=out_tile(grp), row0=grp.row0)
                ks[gi].append(k_new)
                vs[gi].append(v_new)
                hs[gi].append(s_h)
        else:
            o = l // 2
            proj = norm_matmul(x, norm_w[l, 1], gla_w_in, o, tm=tm, tn=1280)
            for gi, grp in enumerate(groups):
                o_g, s_g = _odd_mixer(proj, grp, o, gla_w_gate, gla_b_gate, gla_norm_w)
                x = out_proj(x, o_g, 0, o_g, 1, gla_w_out, o, tm=out_tile(grp), row0=grp.row0)
                gs[gi].append(s_g)
        x = ffn(x, norm_w[l, 2], ffn_gate, ffn_up, ffn_down, l, 1, final_norm_w if l == depth - 1 else None, tm=tm)
    return x, ks, vs, hs, gs


def kernel(x_prompt, x_sample, cache_k, cache_v, state_hgrn, state_gla, page_table, norm_w, ffn_gate, ffn_up,
           ffn_down, even_w_in, even_w_out, hgrn_lb_logits, hgrn_norm_w, gla_w_in, gla_w_gate_up, gla_b_gate,
           gla_norm_w, gla_w_out, final_norm_w):
    lb = jnp.cumsum(jax.nn.softmax(hgrn_lb_logits.astype(F32), axis=0), axis=0)
    lb = lb - lb[0:1]

    n_odd, d_model, odd_in = gla_w_in.shape
    gate_col = odd_in - GLA_GATE_RANK
    assert gate_col % LANES == 0
    odd_cols = -(-(gate_col + LANES) // (5 * MXU_DIM)) * (5 * MXU_DIM)
    gla_w_in_p = jnp.pad(gla_w_in, ((0, 0), (0, 0), (0, odd_cols - odd_in))).astype(BF16)
    gla_w_gate_p = jnp.pad(gla_w_gate_up, ((0, 0), (0, LANES - GLA_GATE_RANK), (0, 0))).astype(BF16)
    wts = (norm_w, ffn_gate, ffn_up, ffn_down, even_w_in.astype(BF16), even_w_out.astype(BF16), hgrn_norm_w,
           gla_w_in_p, gla_w_gate_p, gla_b_gate, gla_norm_w, gla_w_out.astype(BF16), final_norm_w)

    Bp, Lp, D = x_prompt.shape
    Bs, Ls, _ = x_sample.shape
    n_even = state_hgrn.shape[0]
    prompt = _Group(Bp, Lp, 0, jnp.zeros((n_even, Bp) + state_hgrn.shape[2:], F32),
                    jnp.zeros((n_odd, Bp) + state_gla.shape[2:], F32), None)
    sample = _Group(Bs, Ls, Bp * Lp, state_hgrn, state_gla, (cache_k, cache_v, page_table))
    total = Bp * Lp + Bs * Ls
    x = jnp.pad(x_prompt.reshape(Bp * Lp, D), ((0, Bs * Ls + -total % LANES), (0, 0)))
    x = lax.dynamic_update_slice(x, x_sample.reshape(Bs * Ls, D), (Bp * Lp, 0))
    y, ks, vs, hs, gs = _trunk(x, (prompt, sample), lb, wts)

    pages = lambda a: a.reshape(Bp, Lp // PAGE_SIZE, PAGE_SIZE, MOBA_HEADS, HEAD_DIM)
    rows = lambda a: a.reshape(Bs, Ls, MOBA_HEADS, HEAD_DIM)
    return (y[:Bp * Lp].reshape(Bp, Lp, D), y[Bp * Lp:total].reshape(Bs, Ls, D),
            jnp.stack([pages(a) for a in ks[0]]), jnp.stack([pages(a) for a in vs[0]]),
            jnp.stack(hs[0]), jnp.stack(gs[0]),
            jnp.stack([rows(a) for a in ks[1]]), jnp.stack([rows(a) for a in vs[1]]),
            jnp.stack(hs[1]), jnp.stack(gs[1]))
```

```python
import functools
from typing import NamedTuple, Optional

import numpy as np
import jax
import jax.numpy as jnp
from jax import lax
from jax.experimental import pallas as pl
from jax.experimental.pallas import tpu as pltpu

F32 = jnp.float32
BF16 = jnp.bfloat16

NORM_EPS = 1e-6
HEAD_DIM = 128
HGRN_HEADS = 8
MOBA_HEADS = 8
MOBA_BLOCK = 256
MOBA_TOPK = 3
PAGE_SIZE = 128
GLA_HEADS = 4
GLA_GATE_RANK = 16
GLA_GATE_NORM = 16.0

LANES = 128
MXU_DIM = 256
REC_CHUNK = 128
REC_LEVELS = 7
REC_SMALL_HALVES = (4, 2)
MOBA_KEY_TILE = 4
MOBA_PAGE_GROUP = 8
MASKED = -1e30
VMEM_LIMIT = 60 * 1024 * 1024

NT_DIMS = (((1,), (1,)), ((), ()))
TN_DIMS = (((0,), (0,)), ((), ()))
NEG_INF = float("-inf")


def _params(*sem):
    return pltpu.CompilerParams(dimension_semantics=sem, vmem_limit_bytes=VMEM_LIMIT)


def _rms(x, w):
    return x * lax.rsqrt(jnp.mean(x * x, axis=-1, keepdims=True) + NORM_EPS) * w


def _silu(x):
    return x * jax.nn.sigmoid(x)


def _bdot(a, b):
    return jnp.dot(a.astype(BF16), b.astype(BF16), preferred_element_type=F32)


def _bdot_nt(a, b):
    return lax.dot_general(a.astype(BF16), b.astype(BF16), NT_DIMS, preferred_element_type=F32)


def _ffn_body(x_ref, nw_ref, wg_ref, wu_ref, wd_ref, *rest, final):
    if final:
        fw_ref, o_ref, h_ref, a_ref = rest
    else:
        o_ref, h_ref, a_ref = rest
    j = pl.program_id(1)
    last = pl.num_programs(1) - 1

    def hidden():
        h = h_ref[...]
        g = jnp.dot(h, wg_ref[...].astype(BF16), preferred_element_type=F32)
        u = jnp.dot(h, wu_ref[...].astype(BF16), preferred_element_type=F32)
        return (_silu(g) * u).astype(BF16)

    def down():
        return jnp.dot(a_ref[...], wd_ref[...].astype(BF16), preferred_element_type=F32)

    @pl.when(j == 0)
    def _():
        h_ref[...] = _rms(x_ref[...], nw_ref[...]).astype(BF16)
        o_ref[...] = jnp.zeros_like(o_ref)
        a_ref[...] = hidden()

    @pl.when((j > 0) & (j < last))
    def _():
        d = down()
        a_ref[...] = hidden()
        o_ref[...] += d

    @pl.when(j == last)
    def _():
        y = x_ref[...] + 0.5 * (o_ref[...] + down())
        if final:
            y = _rms(y, fw_ref[...])
        o_ref[...] = y


def ffn(x, nw, wg, wu, wd, layer, which, final_w=None, *, tm, tf=256):
    M, D = x.shape
    FF = wg.shape[-1]
    assert M % tm == 0 and FF % tf == 0
    nf = FF // tf
    final = final_w is not None
    up_tile = lambda i, j: (layer, which, 0, jnp.minimum(j, nf - 1))
    in_specs = [
        pl.BlockSpec((tm, D), lambda i, j: (i, 0)),
        pl.BlockSpec((1, D), lambda i, j: (0, 0)),
        pl.BlockSpec((None, None, D, tf), up_tile),
        pl.BlockSpec((None, None, D, tf), up_tile),
        pl.BlockSpec((None, None, tf, D), lambda i, j: (layer, which, jnp.maximum(j - 1, 0), 0)),
    ]
    args = [x, nw.reshape(1, D), wg, wu, wd]
    if final:
        in_specs.append(pl.BlockSpec((1, D), lambda i, j: (0, 0)))
        args.append(final_w.reshape(1, D))
    return pl.pallas_call(
        functools.partial(_ffn_body, final=final),
        grid=(M // tm, nf + 1),
        in_specs=in_specs,
        out_specs=pl.BlockSpec((tm, D), lambda i, j: (i, 0)),
        out_shape=jax.ShapeDtypeStruct((M, D), F32),
        scratch_shapes=[pltpu.VMEM((tm, D), BF16), pltpu.VMEM((tm, tf), BF16)],
        compiler_params=_params("parallel", "arbitrary"),
        name="ffn",
    )(*args)


def _norm_mm_body(x_ref, nw_ref, w_ref, o_ref, h_ref):
    @pl.when(pl.program_id(1) == 0)
    def _():
        h_ref[...] = _rms(x_ref[...], nw_ref[...]).astype(BF16)

    o_ref[...] = jnp.dot(h_ref[...], w_ref[...], preferred_element_type=F32)


def norm_matmul(x, nw, w, layer, *, tm, tn):
    M, D = x.shape
    N = w.shape[-1]
    assert M % tm == 0 and N % tn == 0
    return pl.pallas_call(
        _norm_mm_body,
        grid=(M // tm, N // tn),
        in_specs=[
            pl.BlockSpec((tm, D), lambda i, j: (i, 0)),
            pl.BlockSpec((1, D), lambda i, j: (0, 0)),
            pl.BlockSpec((None, D, tn), lambda i, j: (layer, 0, j)),
        ],
        out_specs=pl.BlockSpec((tm, tn), lambda i, j: (i, j)),
        out_shape=jax.ShapeDtypeStruct((M, N), F32),
        scratch_shapes=[pltpu.VMEM((tm, D), BF16)],
        compiler_params=_params("parallel", "arbitrary"),
        name="norm_matmul",
    )(x, nw.reshape(1, D), w)


def _out_proj_body(x_ref, a_ref, b_ref, wa_ref, wb_ref, o_ref):
    o_ref[...] = x_ref[...] + _bdot(a_ref[...], wa_ref[...]) + _bdot(b_ref[...], wb_ref[...])


def out_proj(x, a, a_blk, b, b_blk, w, layer, *, tm, row0):
    D = x.shape[1]
    rows = a.shape[0]
    K = w.shape[1] // 2
    assert rows % tm == 0 and row0 % tm == 0
    first = row0 // tm
    return pl.pallas_call(
        _out_proj_body,
        grid=(rows // tm,),
        in_specs=[
            pl.BlockSpec((tm, D), lambda i: (first + i, 0)),
            pl.BlockSpec((tm, K), lambda i: (i, a_blk)),
            pl.BlockSpec((tm, K), lambda i: (i, b_blk)),
            pl.BlockSpec((None, K, D), lambda i: (layer, 0, 0)),
            pl.BlockSpec((None, K, D), lambda i: (layer, 1, 0)),
        ],
        out_specs=pl.BlockSpec((tm, D), lambda i: (first + i, 0)),
        out_shape=jax.ShapeDtypeStruct(x.shape, F32),
        input_output_aliases={0: 0},
        compiler_params=_params("parallel"),
        name="out_proj",
    )(x, a, b, w, w)


def _rec_constants():
    C = REC_CHUNK
    t = np.arange(C)[:, None]
    s = np.arange(C)[None, :]
    mats = [(s <= t).astype(np.float32)]
    a_mats, b_mats = [], []
    level = np.full((C, C), -1, np.int32)
    level[np.arange(C), np.arange(C)] = REC_LEVELS
    for li in range(REC_LEVELS):
        L = C >> (li + 1)
        mid = (t // (2 * L)) * (2 * L) + L
        if L in REC_SMALL_HALVES:
            a_mats.append(((t >= mid) & (s >= mid) & (s <= t)).astype(np.float32))
            b_mats.append(((t < mid) & (s >= t + 1) & (s <= mid - 1)).astype(np.float32))
        mid_s = (s // (2 * L)) * (2 * L) + L
        level[(t >= mid) & (s < mid_s) & (mid_s == mid)] = li
    w = np.concatenate(mats + a_mats + b_mats, axis=0)
    return jnp.asarray(w, BF16), jnp.asarray(level)


def _level_factors(li, q, k, b, g, e):
    C = REC_CHUNK
    L = C >> (li + 1)
    if L == 1:
        return q * jnp.exp(g), k
    if L in REC_SMALL_HALVES:
        i = REC_SMALL_HALVES.index(L)
        n = len(REC_SMALL_HALVES)
        return q * jnp.exp(e[(1 + i) * C:(2 + i) * C]), k * jnp.exp(e[(1 + n + i) * C:(2 + n + i) * C])
    qs, ks = [], []
    for m in range(0, C, 2 * L):
        c = b[m + L - 1:m + L, :]
        left, right = slice(m, m + L), slice(m + L, m + 2 * L)
        qs += [q[left], q[right] * jnp.exp(b[right] - c)]
        ks += [k[left] * jnp.exp(c - b[left]), k[right]]
    return jnp.concatenate(qs, axis=0), jnp.concatenate(ks, axis=0)


def _rec_core(q, k, vs, g, w_ref, level, st_ref):
    C = REC_CHUNK
    heads = len(vs)
    dk = q.shape[1] // heads
    head = lambda a, h: a[:, h * dk:(h + 1) * dk]
    g1 = g.astype(BF16)
    r1 = g - g1.astype(F32)
    g2 = r1.astype(BF16)
    g3 = (r1 - g2.astype(F32)).astype(BF16)
    w = w_ref[...]
    e = (jnp.dot(w, g1, preferred_element_type=F32) + jnp.dot(w, g2, preferred_element_type=F32)
         + jnp.dot(w, g3, preferred_element_type=F32))
    b = e[0:C]
    b_last = b[C - 1:C, :]
    sts = [st_ref[h] for h in range(heads)]
    qb = (q * jnp.exp(b)).astype(BF16)
    inters = [lax.dot_general(head(qb, h), sts[h].astype(BF16), NT_DIMS, preferred_element_type=F32)
              for h in range(heads)]
    atts = [jnp.where(level == REC_LEVELS, _bdot_nt(head(q, h), head(k, h)), 0.0) for h in range(heads)]
    for li in range(REC_LEVELS):
        ql, kl = _level_factors(li, q, k, b, g, e)
        ql, kl = ql.astype(BF16), kl.astype(BF16)
        for h in range(heads):
            part = lax.dot_general(head(ql, h), head(kl, h), NT_DIMS, preferred_element_type=F32)
            atts[h] = jnp.where(level == li, part, atts[h])
    kd = (k * jnp.exp(b_last - b)).astype(BF16)
    decay = jnp.exp(b_last)
    outs = []
    for h in range(heads):
        outs.append(inters[h] + _bdot(atts[h], vs[h]))
        upd = lax.dot_general(vs[h].astype(BF16), head(kd, h), TN_DIMS, preferred_element_type=F32)
        st_ref[h] = sts[h] * head(decay, h) + upd
    return outs


def _valid_rows(shape, valid_len):
    return lax.broadcasted_iota(jnp.int32, shape, 0) < valid_len


def _load_state(s0_ref, st_ref):
    @pl.when(pl.program_id(2) == 0)
    def _():
        for hh in range(st_ref.shape[0]):
            st_ref[hh] = s0_ref[hh].T


def _store_state(so_ref, st_ref):
    @pl.when(pl.program_id(2) == pl.num_programs(2) - 1)
    def _():
        for hh in range(st_ref.shape[0]):
            so_ref[hh] = st_ref[hh].T


def _hgrn_body(hq_ref, hf_ref, hi_ref, hg_ref, lb_ref, nw_ref, s0_ref, w_ref, lvl_ref, o_ref, so_ref, st_ref,
               *, valid_len):
    _load_state(s0_ref, st_ref)
    heads, dv, dk = st_ref.shape
    lb = lb_ref[...]
    f = lb + (1.0 - lb) * jax.nn.sigmoid(hf_ref[...])
    k = 1.0 - f
    g = jnp.log(f)
    if valid_len < REC_CHUNK:
        ok = _valid_rows(g.shape, valid_len)
        k = jnp.where(ok, k, 0.0)
        g = jnp.where(ok, g, 0.0)
    q = _silu(hq_ref[...]) * (dk ** -0.5)
    hs = range(heads)
    outs = _rec_core(q, k, [hi_ref[:, h * dv:(h + 1) * dv] for h in hs], g, w_ref, lvl_ref[...], st_ref)
    for h in hs:
        o_ref[:, h * dv:(h + 1) * dv] = _rms(outs[h], nw_ref[...]) * _silu(hg_ref[:, h * dv:(h + 1) * dv])
    _store_state(so_ref, st_ref)


def hgrn_mix(proj, lb, norm_w, s0, *, batch, seq, valid_len=REC_CHUNK, heads_per_step=8):
    H, dk, dv = s0.shape[1:]
    C = REC_CHUNK
    hb = heads_per_step
    assert H % hb == 0
    n = seq // C
    wmat, level = _rec_constants()
    row = lambda b, h, c: b * n + c
    col_spec = lambda grp: pl.BlockSpec((C, hb * dk), lambda b, h, c: (row(b, h, c), grp * (H // hb) + h))
    const = lambda shape: pl.BlockSpec(shape, lambda b, h, c: (0,) * len(shape))
    state_spec = pl.BlockSpec((None, hb, dk, dv), lambda b, h, c: (b, h, 0, 0))
    return pl.pallas_call(
        functools.partial(_hgrn_body, valid_len=valid_len),
        grid=(batch, H // hb, n),
        in_specs=[col_spec(0), col_spec(1), col_spec(2), col_spec(3),
                  pl.BlockSpec((1, hb * dk), lambda b, h, c: (0, h)),
                  const((1, dv)), state_spec, const(wmat.shape), const(level.shape)],
        out_specs=[pl.BlockSpec((C, hb * dv), lambda b, h, c: (row(b, h, c), h)), state_spec],
        out_shape=[jax.ShapeDtypeStruct((batch * n * C, H * dv), F32), jax.ShapeDtypeStruct(s0.shape, F32)],
        scratch_shapes=[pltpu.VMEM((hb, dv, dk), F32)],
        compiler_params=_params("parallel", "parallel", "arbitrary"),
        name="hgrn_mix",
    )(proj, proj, proj, proj, lb.reshape(1, H * dk), norm_w.reshape(1, dv), s0, wmat, level)


def _gla_body(q_ref, k_ref, v_ref, r_ref, a_ref, wg_ref, bg_ref, nw_ref, s0_ref, w_ref, lvl_ref, o_ref, so_ref,
              st_ref, *, valid_len):
    _load_state(s0_ref, st_ref)
    heads, dv, dk = st_ref.shape
    x = _bdot(a_ref[...], wg_ref[...]) + bg_ref[...]
    g = -(jnp.maximum(-x, 0.0) + jnp.log(1.0 + jnp.exp(-jnp.abs(x)))) / GLA_GATE_NORM
    k = k_ref[...]
    if valid_len < REC_CHUNK:
        ok = _valid_rows(g.shape, valid_len)
        k = jnp.where(ok, k, 0.0)
        g = jnp.where(ok, g, 0.0)
    q = q_ref[...] * (dk ** -0.5)
    hs = range(heads)
    outs = _rec_core(q, k, [v_ref[:, h * dv:(h + 1) * dv] for h in hs], g, w_ref, lvl_ref[...], st_ref)
    for h in hs:
        o_ref[:, h * dv:(h + 1) * dv] = _rms(outs[h], nw_ref[...]) * _silu(r_ref[:, h * dv:(h + 1) * dv])
    _store_state(so_ref, st_ref)


def gla_mix(proj, w_gate, layer, b_gate, norm_w, s0, *, batch, seq, valid_len=REC_CHUNK, heads_per_step=4):
    H, dk, dv = s0.shape[1:]
    C = REC_CHUNK
    hb = heads_per_step
    assert H % hb == 0
    n = seq // C
    nh = H // hb
    wmat, level = _rec_constants()
    row = lambda b, h, c: b * n + c
    const = lambda shape: pl.BlockSpec(shape, lambda b, h, c: (0,) * len(shape))
    state_spec = pl.BlockSpec((None, hb, dk, dv), lambda b, h, c: (b, h, 0, 0))
    kv_off = 2 * H * dk // (hb * dv)
    a_blk = (2 * H * dk + 2 * H * dv) // LANES
    return pl.pallas_call(
        functools.partial(_gla_body, valid_len=valid_len),
        grid=(batch, nh, n),
        in_specs=[pl.BlockSpec((C, hb * dk), lambda b, h, c: (row(b, h, c), h)),
                  pl.BlockSpec((C, hb * dk), lambda b, h, c: (row(b, h, c), nh + h)),
                  pl.BlockSpec((C, hb * dv), lambda b, h, c: (row(b, h, c), kv_off + h)),
                  pl.BlockSpec((C, hb * dv), lambda b, h, c: (row(b, h, c), kv_off + nh + h)),
                  pl.BlockSpec((C, LANES), lambda b, h, c: (row(b, h, c), a_blk)),
                  pl.BlockSpec((None, LANES, hb * dk), lambda b, h, c: (layer, 0, h)),
                  pl.BlockSpec((1, hb * dk), lambda b, h, c: (0, h)),
                  const((1, dv)), state_spec, const(wmat.shape), const(level.shape)],
        out_specs=[pl.BlockSpec((C, hb * dv), lambda b, h, c: (row(b, h, c), h)), state_spec],
        out_shape=[jax.ShapeDtypeStruct((batch * n * C, H * dv), F32), jax.ShapeDtypeStruct(s0.shape, F32)],
        scratch_shapes=[pltpu.VMEM((hb, dv, dk), F32)],
        compiler_params=_params("parallel", "parallel", "arbitrary"),
        name="gla_mix",
    )(proj, proj, proj, proj, proj, w_gate, b_gate.reshape(1, H * dk), norm_w.reshape(1, dv), s0, wmat, level)


def _top_blocks(s, index, axis):
    sel = jnp.zeros(s.shape, F32)
    for _ in range(MOBA_TOPK):
        m = jnp.max(s, axis=axis, keepdims=True)
        idx = jnp.min(jnp.where(s == m, index, s.shape[axis]), axis=axis, keepdims=True)
        hit = index == idx
        sel = jnp.where(hit & (m > NEG_INF), 1.0, sel)
        s = jnp.where(hit, NEG_INF, s)
    return sel


def _moba_prompt_body(q_ref, k_ref, v_ref, o_ref, kmean_ref, kaug_ref, vt_ref, lg_ref):
    i = pl.program_id(2)
    blk = MOBA_BLOCK
    tile = MOBA_KEY_TILE * blk
    T = k_ref.shape[0]
    nb = T // blk
    heads, nbp, dh = kmean_ref.shape
    hs = range(heads)
    cols = lambda h: slice(h * dh, (h + 1) * dh)

    @pl.when(i == 0)
    def _():
        kmean_ref[...] = jnp.zeros_like(kmean_ref)
        key_blk = lax.broadcasted_iota(jnp.int32, (T, LANES), 0) // blk
        on_blk = lax.broadcasted_iota(jnp.int32, (T, LANES), 1) == key_blk
        for h in hs:
            for jb in range(nb):
                rows = slice(jb * blk, (jb + 1) * blk)
                kmean_ref[h, jb:jb + 1, :] = jnp.mean(k_ref[rows, cols(h)], axis=0, keepdims=True)
                vt_ref[h, :, rows] = v_ref[rows, cols(h)].T.astype(BF16)
            kaug_ref[h, :, 0:dh] = k_ref[:, cols(h)].astype(BF16)
            kaug_ref[h, :, dh:dh + LANES] = jnp.where(on_blk, MASKED, 0.0).astype(BF16)

    own = pl.multiple_of(i * blk, blk)
    blk_id = lax.broadcasted_iota(jnp.int32, (nbp, blk), 0)
    visible = lax.broadcasted_iota(jnp.int32, (blk, blk), 0) <= lax.broadcasted_iota(jnp.int32, (blk, blk), 1)
    qaugs, m0 = [], []
    for h in hs:
        q = q_ref[:, cols(h)]
        s = lax.dot_general(kmean_ref[h], q, NT_DIMS, precision=lax.Precision.HIGHEST, preferred_element_type=F32)
        sel = _top_blocks(jnp.where(blk_id < i, s, NEG_INF), blk_id, 0)
        qst = (q * (dh ** -0.5)).T.astype(BF16)
        unpicked = jnp.concatenate([1.0 - sel, jnp.zeros((LANES - nbp, blk), F32)], axis=0).astype(BF16)
        qaugs.append(jnp.concatenate([qst, unpicked], axis=0))
        lg_own = jnp.dot(kaug_ref[h, pl.ds(own, blk), 0:dh], qst, preferred_element_type=F32)
        lg_own = jnp.where(visible, lg_own, NEG_INF)
        lg_ref[h, T:T + blk, :] = lg_own
        m0.append(jnp.max(lg_own, axis=0, keepdims=True))
    n_tiles = (i + MOBA_KEY_TILE - 1) // MOBA_KEY_TILE

    def score(t, ms):
        start = pl.multiple_of(t * tile, tile)
        out = []
        for h in hs:
            lg = jnp.dot(kaug_ref[h, pl.ds(start, tile), :], qaugs[h], preferred_element_type=F32)
            lg_ref[h, pl.ds(start, tile), :] = lg
            out.append(jnp.maximum(ms[h], jnp.max(lg, axis=0, keepdims=True)))
        return tuple(out)

    n_pairs = n_tiles // 2
    pair = lambda body: (lambda tp, carry: body(2 * tp + 1, body(2 * tp, carry)))
    ms = lax.fori_loop(0, n_pairs, pair(score), tuple(m0))
    ms = lax.fori_loop(2 * n_pairs, n_tiles, score, ms)

    init = []
    for h in hs:
        p_own = jnp.exp(lg_ref[h, T:T + blk, :] - ms[h])
        acc0 = jnp.dot(vt_ref[h, :, pl.ds(own, blk)], p_own.astype(BF16), preferred_element_type=F32)
        init.append((jnp.sum(p_own, axis=0, keepdims=True), acc0))

    def gather(t, carry):
        start = pl.multiple_of(t * tile, tile)
        out = []
        for h in hs:
            l, acc = carry[h]
            p = jnp.exp(lg_ref[h, pl.ds(start, tile), :] - ms[h])
            acc = acc + jnp.dot(vt_ref[h, :, pl.ds(start, tile)], p.astype(BF16), preferred_element_type=F32)
            out.append((l + jnp.sum(p, axis=0, keepdims=True), acc))
        return tuple(out)

    carry = lax.fori_loop(0, n_pairs, pair(gather), tuple(init))
    carry = lax.fori_loop(2 * n_pairs, n_tiles, gather, carry)
    for h in hs:
        l, acc = carry[h]
        o_ref[:, cols(h)] = (acc / l).T


def moba_prompt(proj, *, batch, seq, q_col, k_col, v_col, heads_per_step=2):
    H, dh, blk = MOBA_HEADS, HEAD_DIM, MOBA_BLOCK
    T = seq
    hb = heads_per_step
    assert T % (MOBA_KEY_TILE * blk) == 0 and T // blk <= LANES
    assert H % hb == 0 and q_col % hb == 0 and k_col % hb == 0 and v_col % hb == 0
    nq = T // blk
    nbp = -(-nq // 8) * 8
    return pl.pallas_call(
        _moba_prompt_body,
        grid=(batch, H // hb, nq),
        in_specs=[pl.BlockSpec((blk, hb * dh), lambda b, h, i: (b * nq + i, q_col // hb + h)),
                  pl.BlockSpec((T, hb * dh), lambda b, h, i: (b, k_col // hb + h)),
                  pl.BlockSpec((T, hb * dh), lambda b, h, i: (b, v_col // hb + h))],
        out_specs=pl.BlockSpec((blk, hb * dh), lambda b, h, i: (b * nq + i, h)),
        out_shape=jax.ShapeDtypeStruct((batch * T, H * dh), F32),
        scratch_shapes=[pltpu.VMEM((hb, nbp, dh), F32), pltpu.VMEM((hb, T, dh + LANES), BF16),
                        pltpu.VMEM((hb, dh, T), BF16), pltpu.VMEM((hb, T + blk, blk), F32)],
        compiler_params=_params("parallel", "parallel", "arbitrary"),
        name="moba_prompt",
    )(proj, proj, proj)


def _stack_heads(q):
    heads = q.shape[1] // HEAD_DIM
    return jnp.concatenate([q[:, h * HEAD_DIM:(h + 1) * HEAD_DIM] for h in range(heads)], axis=0)


def _same_head(rows, cols, lq, heads):
    r = lax.broadcasted_iota(jnp.int32, (rows, cols), 0)
    c = lax.broadcasted_iota(jnp.int32, (rows, cols), 1)
    return (r // lq) == (c % heads)


def _moba_pages_body(pt_ref, q_ref, *refs):
    group = MOBA_PAGE_GROUP
    k_refs, v_refs = refs[:group], refs[group:2 * group]
    sc_ref, m_ref, l_ref, o_ref = refs[2 * group:]
    step = pl.program_id(1)
    lq = q_ref.shape[0]
    n_tok, heads, dh = k_refs[0].shape

    @pl.when(step == 0)
    def _():
        sc_ref[...] = jnp.zeros_like(sc_ref)
        m_ref[...] = jnp.zeros_like(m_ref)
        l_ref[...] = jnp.zeros_like(l_ref)

    qa = _stack_heads(q_ref[...])
    rows = qa.shape[0]
    qs = (qa * (dh ** -0.5)).astype(BF16)
    same = _same_head(rows, n_tok * heads, lq, heads)
    lane = lax.broadcasted_iota(jnp.int32, sc_ref.shape, 1)
    sc_all, m_all, l_all = sc_ref[...], m_ref[...], l_ref[...]
    pages = range(group)
    logits = [lax.dot_general(qs, k_refs[gi][...].reshape(n_tok * heads, dh).astype(BF16), NT_DIMS,
                              preferred_element_type=F32) for gi in pages]
    logits = [jnp.where(same, lg, NEG_INF) for lg in logits]
    ms = [jnp.max(lg, axis=1, keepdims=True) for lg in logits]
    prs = [jnp.exp(logits[gi] - ms[gi]) for gi in pages]
    for gi in pages:
        o_ref[gi] = _bdot(prs[gi], v_refs[gi][...].reshape(n_tok * heads, dh))
    for gi in pages:
        ksum = jnp.sum(k_refs[gi][...], axis=0)
        ksum_rows = jnp.concatenate([jnp.broadcast_to(ksum[h:h + 1, :], (lq, dh)) for h in range(heads)], axis=0)
        here = lane == step * group + gi
        sc_all = jnp.where(here, jnp.sum(qa * ksum_rows, axis=1, keepdims=True), sc_all)
        m_all = jnp.where(here, ms[gi], m_all)
        l_all = jnp.where(here, jnp.sum(prs[gi], axis=1, keepdims=True), l_all)
    sc_ref[...] = sc_all
    m_ref[...] = m_all
    l_ref[...] = l_all


def moba_pages(q, cache_k, cache_v, page_table, layer):
    B, lq, width = q.shape
    _, _, n_tok, heads, dh = cache_k.shape
    n_pages = page_table.shape[1]
    group = MOBA_PAGE_GROUP
    assert n_pages == LANES and heads * dh == width and n_pages % group == 0
    rows = heads * lq
    page_spec = lambda gi: pl.BlockSpec((None, None, n_tok, heads, dh),
                                        lambda b, p, pt: (layer, pt[b, p * group + gi], 0, 0, 0))
    stat_spec = pl.BlockSpec((None, rows, LANES), lambda b, p, pt: (b, 0, 0))
    stat = jax.ShapeDtypeStruct((B, rows, LANES), F32)
    page_specs = [page_spec(gi) for gi in range(group)]
    return pl.pallas_call(
        _moba_pages_body,
        grid_spec=pltpu.PrefetchScalarGridSpec(
            num_scalar_prefetch=1,
            grid=(B, n_pages // group),
            in_specs=[pl.BlockSpec((None, lq, width), lambda b, p, pt: (b, 0, 0))] + page_specs + page_specs,
            out_specs=[stat_spec, stat_spec, stat_spec,
                       pl.BlockSpec((None, group, rows, dh), lambda b, p, pt: (b, p, 0, 0))],
        ),
        out_shape=[stat, stat, stat, jax.ShapeDtypeStruct((B, n_pages, rows, dh), F32)],
        compiler_params=_params("parallel", "arbitrary"),
        name="moba_pages",
    )(page_table, q, *([cache_k] * group), *([cache_v] * group))


def _moba_combine_body(sc_ref, m_ref, l_ref, op_ref, q_ref, kn_ref, vn_ref, o_ref):
    lq = q_ref.shape[0]
    rows = sc_ref.shape[0]
    heads = rows // lq
    n_pages = op_ref.shape[0]
    pages_per_block = MOBA_BLOCK // PAGE_SIZE
    assert pages_per_block == 2
    lane = lax.broadcasted_iota(jnp.int32, (rows, LANES), 1)
    sc = sc_ref[...]
    bs = (sc + pltpu.roll(sc, LANES - 1, 1)) * (1.0 / MOBA_BLOCK)
    sel = _top_blocks(jnp.where(lane % pages_per_block == 0, bs, NEG_INF), lane, 1)
    sel = sel + pltpu.roll(sel, 1, 1)

    qa = _stack_heads(q_ref[...])
    lo = _bdot_nt(qa * (HEAD_DIM ** -0.5), kn_ref[...])
    t_q = lax.broadcasted_iota(jnp.int32, (rows, LANES), 0) % lq
    lo = jnp.where(_same_head(rows, LANES, lq, heads) & (lane // heads <= t_q), lo, NEG_INF)
    m = m_ref[...]
    m_all = jnp.maximum(jnp.max(lo, axis=1, keepdims=True),
                        jnp.max(jnp.where(sel > 0.0, m, NEG_INF), axis=1, keepdims=True))
    w = jnp.where(sel > 0.0, jnp.exp(m - m_all), 0.0)
    p_own = jnp.exp(lo - m_all)
    den = jnp.sum(w * l_ref[...], axis=1, keepdims=True) + jnp.sum(p_own, axis=1, keepdims=True)
    num0 = _bdot(p_own, vn_ref[...])

    num = num0
    for p in range(n_pages):
        num = num + w[:, p:p + 1] * op_ref[p]
    out = num / den
    o_ref[...] = jnp.concatenate([out[h * lq:(h + 1) * lq, :] for h in range(heads)], axis=1)


def moba_combine(sc, m, l, o_pages, q, k_new, v_new):
    B, lq, width = q.shape
    heads, dh = k_new.shape[2:]
    rows = sc.shape[1]
    n_pages = o_pages.shape[1]
    assert LANES % heads == 0 and lq * heads <= LANES
    pad = ((0, 0), (0, LANES // heads - lq), (0, 0), (0, 0))
    kn = jnp.pad(k_new, pad).reshape(B, LANES, dh)
    vn = jnp.pad(v_new, pad).reshape(B, LANES, dh)
    stat_spec = pl.BlockSpec((None, rows, LANES), lambda b: (b, 0, 0))
    new_spec = pl.BlockSpec((None, LANES, dh), lambda b: (b, 0, 0))
    q_spec = pl.BlockSpec((None, lq, width), lambda b: (b, 0, 0))
    return pl.pallas_call(
        _moba_combine_body,
        grid=(B,),
        in_specs=[stat_spec, stat_spec, stat_spec,
                  pl.BlockSpec((None, n_pages, rows, dh), lambda b: (b, 0, 0, 0)),
                  q_spec, new_spec, new_spec],
        out_specs=q_spec,
        out_shape=jax.ShapeDtypeStruct((B, lq, width), F32),
        compiler_params=_params("parallel"),
        name="moba_combine",
    )(sc, m, l, o_pages, q, kn, vn)


def _pad_tokens(a, batch, length):
    a = a.reshape(batch, -1, a.shape[-1])
    return jnp.pad(a, ((0, 0), (0, length - a.shape[1]), (0, 0))).reshape(batch * length, a.shape[-1])


def _unpad_tokens(a, batch, seq):
    return a.reshape(batch, -1, a.shape[-1])[:, :seq].reshape(batch * seq, a.shape[-1])


def _row_tile(rows, target):
    n = max(1, round(rows / target))
    assert rows % n == 0 and (rows // n) % 8 == 0, rows
    return rows // n


class _Group(NamedTuple):
    batch: int
    seq: int
    row0: int
    hgrn_s0: jax.Array
    gla_s0: jax.Array
    past: Optional[tuple]


def _group_rows(proj, grp):
    if grp.seq % REC_CHUNK == 0:
        assert grp.row0 == 0
        return proj, grp.seq, REC_CHUNK
    own = proj[grp.row0:grp.row0 + grp.batch * grp.seq]
    seq = -(-grp.seq // REC_CHUNK) * REC_CHUNK
    assert seq == REC_CHUNK
    return _pad_tokens(own, grp.batch, seq), seq, grp.seq


def _even_mixer(proj, grp, e, lb, hgrn_norm_w):
    hw = HGRN_HEADS * HEAD_DIM
    mw = MOBA_HEADS * HEAD_DIM
    rows = grp.batch * grp.seq
    rec_in, seq, valid = _group_rows(proj[:, :4 * hw] if grp.seq % REC_CHUNK else proj, grp)
    o_h, s_h = hgrn_mix(rec_in, lb[e], hgrn_norm_w[e], grp.hgrn_s0[e], batch=grp.batch, seq=seq, valid_len=valid)
    o_h = _unpad_tokens(o_h, grp.batch, grp.seq)
    own = proj[grp.row0:grp.row0 + rows]
    k_new = own[:, 4 * hw + mw:4 * hw + 2 * mw]
    v_new = own[:, 4 * hw + 2 * mw:4 * hw + 3 * mw]
    if grp.past is None:
        assert grp.row0 == 0
        first = 4 * hw // HEAD_DIM
        o_m = moba_prompt(proj, batch=grp.batch, seq=grp.seq, q_col=first, k_col=first + MOBA_HEADS,
                          v_col=first + 2 * MOBA_HEADS)
    else:
        cache_k, cache_v, page_table = grp.past
        q = own[:, 4 * hw:4 * hw + mw].reshape(grp.batch, grp.seq, mw)
        sc, m, lsum, o_pages = moba_pages(q, cache_k, cache_v, page_table, e)
        heads = lambda a: a.reshape(grp.batch, grp.seq, MOBA_HEADS, HEAD_DIM)
        o_m = moba_combine(sc, m, lsum, o_pages, q, heads(k_new), heads(v_new)).reshape(rows, mw)
    return o_h, o_m, s_h, k_new, v_new


def _odd_mixer(proj, grp, o, gla_w_gate, gla_b_gate, gla_norm_w):
    rec_in, seq, valid = _group_rows(proj, grp)
    o_g, s_g = gla_mix(rec_in, gla_w_gate, o, gla_b_gate[o], gla_norm_w[o], grp.gla_s0[o], batch=grp.batch, seq=seq,
                       valid_len=valid)
    return _unpad_tokens(o_g, grp.batch, grp.seq), s_g


def _trunk(x, groups, lb, wts):
    (norm_w, ffn_gate, ffn_up, ffn_down, even_w_in, even_w_out, hgrn_norm_w, gla_w_in, gla_w_gate,
     gla_b_gate, gla_norm_w, gla_w_out, final_norm_w) = wts
    depth = norm_w.shape[0]
    tm = _row_tile(x.shape[0], 1024)
    out_tile = lambda grp: _row_tile(grp.batch * grp.seq, 512)
    ks, vs, hs, gs = ([[] for _ in groups] for _ in range(4))
    for l in range(depth):
        x = ffn(x, norm_w[l, 0], ffn_gate, ffn_up, ffn_down, l, 0, tm=tm)
        if l % 2 == 0:
            e = l // 2
            proj = norm_matmul(x, norm_w[l, 1], even_w_in, e, tm=tm, tn=1024)
            for gi, grp in enumerate(groups):
                o_h, o_m, s_h, k_new, v_new = _even_mixer(proj, grp, e, lb, hgrn_norm_w)
                x = out_proj(x, o_h, 0, o_m, 0, even_w_out, e, tm=out_tile(grp), row0=grp.row0)
                ks[gi].append(k_new)
                vs[gi].append(v_new)
                hs[gi].append(s_h)
        else:
            o = l // 2
            proj = norm_matmul(x, norm_w[l, 1], gla_w_in, o, tm=tm, tn=1280)
            for gi, grp in enumerate(groups):
                o_g, s_g = _odd_mixer(proj, grp, o, gla_w_gate, gla_b_gate, gla_norm_w)
                x = out_proj(x, o_g, 0, o_g, 1, gla_w_out, o, tm=out_tile(grp), row0=grp.row0)
                gs[gi].append(s_g)
        x = ffn(x, norm_w[l, 2], ffn_gate, ffn_up, ffn_down, l, 1, final_norm_w if l == depth - 1 else None, tm=tm)
    return x, ks, vs, hs, gs


def kernel(x_prompt, x_sample, cache_k, cache_v, state_hgrn, state_gla, page_table, norm_w, ffn_gate, ffn_up,
           ffn_down, even_w_in, even_w_out, hgrn_lb_logits, hgrn_norm_w, gla_w_in, gla_w_gate_up, gla_b_gate,
           gla_norm_w, gla_w_out, final_norm_w):
    lb = jnp.cumsum(jax.nn.softmax(hgrn_lb_logits.astype(F32), axis=0), axis=0)
    lb = lb - lb[0:1]

    n_odd, d_model, odd_in = gla_w_in.shape
    gate_col = odd_in - GLA_GATE_RANK
    assert gate_col % LANES == 0
    odd_cols = -(-(gate_col + LANES) // (5 * MXU_DIM)) * (5 * MXU_DIM)
    gla_w_in_p = jnp.pad(gla_w_in, ((0, 0), (0, 0), (0, odd_cols - odd_in))).astype(BF16)
    gla_w_gate_p = jnp.pad(gla_w_gate_up, ((0, 0), (0, LANES - GLA_GATE_RANK), (0, 0))).astype(BF16)
    wts = (norm_w, ffn_gate, ffn_up, ffn_down, even_w_in.astype(BF16), even_w_out.astype(BF16), hgrn_norm_w,
           gla_w_in_p, gla_w_gate_p, gla_b_gate, gla_norm_w, gla_w_out.astype(BF16), final_norm_w)

    Bp, Lp, D = x_prompt.shape
    Bs, Ls, _ = x_sample.shape
    n_even = state_hgrn.shape[0]
    prompt = _Group(Bp, Lp, 0, jnp.zeros((n_even, Bp) + state_hgrn.shape[2:], F32),
                    jnp.zeros((n_odd, Bp) + state_gla.shape[2:], F32), None)
    sample = _Group(Bs, Ls, Bp * Lp, state_hgrn, state_gla, (cache_k, cache_v, page_table))
    total = Bp * Lp + Bs * Ls
    x = jnp.pad(x_prompt.reshape(Bp * Lp, D), ((0, Bs * Ls + -total % LANES), (0, 0)))
    x = lax.dynamic_update_slice(x, x_sample.reshape(Bs * Ls, D), (Bp * Lp, 0))
    y, ks, vs, hs, gs = _trunk(x, (prompt, sample), lb, wts)

    pages = lambda a: a.reshape(Bp, Lp // PAGE_SIZE, PAGE_SIZE, MOBA_HEADS, HEAD_DIM)
    rows = lambda a: a.reshape(Bs, Ls, MOBA_HEADS, HEAD_DIM)
    return (y[:Bp * Lp].reshape(Bp, Lp, D), y[Bp * Lp:total].reshape(Bs, Ls, D),
            jnp.stack([pages(a) for a in ks[0]]), jnp.stack([pages(a) for a in vs[0]]),
            jnp.stack(hs[0]), jnp.stack(gs[0]),
            jnp.stack([rows(a) for a in ks[1]]), jnp.stack([rows(a) for a in vs[1]]),
            jnp.stack(hs[1]), jnp.stack(gs[1]))
```

```python
import functools
from typing import NamedTuple, Optional

import numpy as np
import jax
import jax.numpy as jnp
from jax import lax
from jax.experimental import pallas as pl
from jax.experimental.pallas import tpu as pltpu

F32 = jnp.float32
BF16 = jnp.bfloat16

NORM_EPS = 1e-6
HEAD_DIM = 128
HGRN_HEADS = 8
MOBA_HEADS = 8
MOBA_BLOCK = 256
MOBA_TOPK = 3
PAGE_SIZE = 128
GLA_HEADS = 4
GLA_GATE_RANK = 16
GLA_GATE_NORM = 16.0

LANES = 128
MXU_DIM = 256
REC_CHUNK = 128
REC_LEVELS = 7
REC_SMALL_HALVES = (4, 2)
MOBA_KEY_TILE = 4
MOBA_PAGE_GROUP = 16
MASKED = -1e30
VMEM_LIMIT = 60 * 1024 * 1024

NT_DIMS = (((1,), (1,)), ((), ()))
TN_DIMS = (((0,), (0,)), ((), ()))
NEG_INF = float("-inf")


def _params(*sem):
    return pltpu.CompilerParams(dimension_semantics=sem, vmem_limit_bytes=VMEM_LIMIT)


def _rms(x, w):
    return x * lax.rsqrt(jnp.mean(x * x, axis=-1, keepdims=True) + NORM_EPS) * w


def _silu(x):
    return x * jax.nn.sigmoid(x)


def _bdot(a, b):
    return jnp.dot(a.astype(BF16), b.astype(BF16), preferred_element_type=F32)


def _bdot_nt(a, b):
    return lax.dot_general(a.astype(BF16), b.astype(BF16), NT_DIMS, preferred_element_type=F32)


def _ffn_body(x_ref, nw_ref, wg_ref, wu_ref, wd_ref, *rest, final):
    if final:
        fw_ref, o_ref, h_ref, a_ref = rest
    else:
        o_ref, h_ref, a_ref = rest
    j = pl.program_id(1)
    last = pl.num_programs(1) - 1

    def hidden():
        h = h_ref[...]
        g = jnp.dot(h, wg_ref[...].astype(BF16), preferred_element_type=F32)
        u = jnp.dot(h, wu_ref[...].astype(BF16), preferred_element_type=F32)
        return (_silu(g) * u).astype(BF16)

    def down():
        return jnp.dot(a_ref[...], wd_ref[...].astype(BF16), preferred_element_type=F32)

    @pl.when(j == 0)
    def _():
        h_ref[...] = _rms(x_ref[...], nw_ref[...]).astype(BF16)
        o_ref[...] = jnp.zeros_like(o_ref)
        a_ref[...] = hidden()

    @pl.when((j > 0) & (j < last))
    def _():
        d = down()
        a_ref[...] = hidden()
        o_ref[...] += d

    @pl.when(j == last)
    def _():
        y = x_ref[...] + 0.5 * (o_ref[...] + down())
        if final:
            y = _rms(y, fw_ref[...])
        o_ref[...] = y


def ffn(x, nw, wg, wu, wd, layer, which, final_w=None, *, tm, tf=256):
    M, D = x.shape
    FF = wg.shape[-1]
    assert M % tm == 0 and FF % tf == 0
    nf = FF // tf
    final = final_w is not None
    up_tile = lambda i, j: (layer, which, 0, jnp.minimum(j, nf - 1))
    in_specs = [
        pl.BlockSpec((tm, D), lambda i, j: (i, 0)),
        pl.BlockSpec((1, D), lambda i, j: (0, 0)),
        pl.BlockSpec((None, None, D, tf), up_tile),
        pl.BlockSpec((None, None, D, tf), up_tile),
        pl.BlockSpec((None, None, tf, D), lambda i, j: (layer, which, jnp.maximum(j - 1, 0), 0)),
    ]
    args = [x, nw.reshape(1, D), wg, wu, wd]
    if final:
        in_specs.append(pl.BlockSpec((1, D), lambda i, j: (0, 0)))
        args.append(final_w.reshape(1, D))
    return pl.pallas_call(
        functools.partial(_ffn_body, final=final),
        grid=(M // tm, nf + 1),
        in_specs=in_specs,
        out_specs=pl.BlockSpec((tm, D), lambda i, j: (i, 0)),
        out_shape=jax.ShapeDtypeStruct((M, D), F32),
        scratch_shapes=[pltpu.VMEM((tm, D), BF16), pltpu.VMEM((tm, tf), BF16)],
        compiler_params=_params("parallel", "arbitrary"),
        name="ffn",
    )(*args)


def _norm_mm_body(x_ref, nw_ref, w_ref, o_ref, h_ref):
    @pl.when(pl.program_id(1) == 0)
    def _():
        h_ref[...] = _rms(x_ref[...], nw_ref[...]).astype(BF16)

    o_ref[...] = jnp.dot(h_ref[...], w_ref[...], preferred_element_type=F32)


def norm_matmul(x, nw, w, layer, *, tm, tn):
    M, D = x.shape
    N = w.shape[-1]
    assert M % tm == 0 and N % tn == 0
    return pl.pallas_call(
        _norm_mm_body,
        grid=(M // tm, N // tn),
        in_specs=[
            pl.BlockSpec((tm, D), lambda i, j: (i, 0)),
            pl.BlockSpec((1, D), lambda i, j: (0, 0)),
            pl.BlockSpec((None, D, tn), lambda i, j: (layer, 0, j)),
        ],
        out_specs=pl.BlockSpec((tm, tn), lambda i, j: (i, j)),
        out_shape=jax.ShapeDtypeStruct((M, N), F32),
        scratch_shapes=[pltpu.VMEM((tm, D), BF16)],
        compiler_params=_params("parallel", "arbitrary"),
        name="norm_matmul",
    )(x, nw.reshape(1, D), w)


def _out_proj_body(x_ref, a_ref, b_ref, wa_ref, wb_ref, o_ref):
    o_ref[...] = x_ref[...] + _bdot(a_ref[...], wa_ref[...]) + _bdot(b_ref[...], wb_ref[...])


def out_proj(x, a, a_blk, b, b_blk, w, layer, *, tm, row0):
    D = x.shape[1]
    rows = a.shape[0]
    K = w.shape[1] // 2
    assert rows % tm == 0 and row0 % tm == 0
    first = row0 // tm
    return pl.pallas_call(
        _out_proj_body,
        grid=(rows // tm,),
        in_specs=[
            pl.BlockSpec((tm, D), lambda i: (first + i, 0)),
            pl.BlockSpec((tm, K), lambda i: (i, a_blk)),
            pl.BlockSpec((tm, K), lambda i: (i, b_blk)),
            pl.BlockSpec((None, K, D), lambda i: (layer, 0, 0)),
            pl.BlockSpec((None, K, D), lambda i: (layer, 1, 0)),
        ],
        out_specs=pl.BlockSpec((tm, D), lambda i: (first + i, 0)),
        out_shape=jax.ShapeDtypeStruct(x.shape, F32),
        input_output_aliases={0: 0},
        compiler_params=_params("parallel"),
        name="out_proj",
    )(x, a, b, w, w)


def _rec_constants():
    C = REC_CHUNK
    t = np.arange(C)[:, None]
    s = np.arange(C)[None, :]
    mats = [(s <= t).astype(np.float32)]
    a_mats, b_mats = [], []
    level = np.full((C, C), -1, np.int32)
    level[np.arange(C), np.arange(C)] = REC_LEVELS
    for li in range(REC_LEVELS):
        L = C >> (li + 1)
        mid = (t // (2 * L)) * (2 * L) + L
        if L in REC_SMALL_HALVES:
            a_mats.append(((t >= mid) & (s >= mid) & (s <= t)).astype(np.float32))
            b_mats.append(((t < mid) & (s >= t + 1) & (s <= mid - 1)).astype(np.float32))
        mid_s = (s // (2 * L)) * (2 * L) + L
        level[(t >= mid) & (s < mid_s) & (mid_s == mid)] = li
    w = np.concatenate(mats + a_mats + b_mats, axis=0)
    return jnp.asarray(w, BF16), jnp.asarray(level)


def _level_factors(li, q, k, b, g, e):
    C = REC_CHUNK
    L = C >> (li + 1)
    if L == 1:
        return q * jnp.exp(g), k
    if L in REC_SMALL_HALVES:
        i = REC_SMALL_HALVES.index(L)
        n = len(REC_SMALL_HALVES)
        return q * jnp.exp(e[(1 + i) * C:(2 + i) * C]), k * jnp.exp(e[(1 + n + i) * C:(2 + n + i) * C])
    qs, ks = [], []
    for m in range(0, C, 2 * L):
        c = b[m + L - 1:m + L, :]
        left, right = slice(m, m + L), slice(m + L, m + 2 * L)
        qs += [q[left], q[right] * jnp.exp(b[right] - c)]
        ks += [k[left] * jnp.exp(c - b[left]), k[right]]
    return jnp.concatenate(qs, axis=0), jnp.concatenate(ks, axis=0)


def _rec_core(q, k, vs, g, w_ref, level, st_ref):
    C = REC_CHUNK
    heads = len(vs)
    dk = q.shape[1] // heads
    head = lambda a, h: a[:, h * dk:(h + 1) * dk]
    g1 = g.astype(BF16)
    r1 = g - g1.astype(F32)
    g2 = r1.astype(BF16)
    g3 = (r1 - g2.astype(F32)).astype(BF16)
    w = w_ref[...]
    e = (jnp.dot(w, g1, preferred_element_type=F32) + jnp.dot(w, g2, preferred_element_type=F32)
         + jnp.dot(w, g3, preferred_element_type=F32))
    b = e[0:C]
    b_last = b[C - 1:C, :]
    sts = [st_ref[h] for h in range(heads)]
    qb = (q * jnp.exp(b)).astype(BF16)
    inters = [lax.dot_general(head(qb, h), sts[h].astype(BF16), NT_DIMS, preferred_element_type=F32)
              for h in range(heads)]
    atts = [jnp.where(level == REC_LEVELS, _bdot_nt(head(q, h), head(k, h)), 0.0) for h in range(heads)]
    for li in range(REC_LEVELS):
        ql, kl = _level_factors(li, q, k, b, g, e)
        ql, kl = ql.astype(BF16), kl.astype(BF16)
        for h in range(heads):
            part = lax.dot_general(head(ql, h), head(kl, h), NT_DIMS, preferred_element_type=F32)
            atts[h] = jnp.where(level == li, part, atts[h])
    kd = (k * jnp.exp(b_last - b)).astype(BF16)
    decay = jnp.exp(b_last)
    outs = []
    for h in range(heads):
        outs.append(inters[h] + _bdot(atts[h], vs[h]))
        upd = lax.dot_general(vs[h].astype(BF16), head(kd, h), TN_DIMS, preferred_element_type=F32)
        st_ref[h] = sts[h] * head(decay, h) + upd
    return outs


def _valid_rows(shape, valid_len):
    return lax.broadcasted_iota(jnp.int32, shape, 0) < valid_len


def _load_state(s0_ref, st_ref):
    @pl.when(pl.program_id(2) == 0)
    def _():
        for hh in range(st_ref.shape[0]):
            st_ref[hh] = s0_ref[hh].T


def _store_state(so_ref, st_ref):
    @pl.when(pl.program_id(2) == pl.num_programs(2) - 1)
    def _():
        for hh in range(st_ref.shape[0]):
            so_ref[hh] = st_ref[hh].T


def _hgrn_body(hq_ref, hf_ref, hi_ref, hg_ref, lb_ref, nw_ref, s0_ref, w_ref, lvl_ref, o_ref, so_ref, st_ref,
               *, valid_len):
    _load_state(s0_ref, st_ref)
    heads, dv, dk = st_ref.shape
    lb = lb_ref[...]
    f = lb + (1.0 - lb) * jax.nn.sigmoid(hf_ref[...])
    k = 1.0 - f
    g = jnp.log(f)
    if valid_len < REC_CHUNK:
        ok = _valid_rows(g.shape, valid_len)
        k = jnp.where(ok, k, 0.0)
        g = jnp.where(ok, g, 0.0)
    q = _silu(hq_ref[...]) * (dk ** -0.5)
    hs = range(heads)
    outs = _rec_core(q, k, [hi_ref[:, h * dv:(h + 1) * dv] for h in hs], g, w_ref, lvl_ref[...], st_ref)
    for h in hs:
        o_ref[:, h * dv:(h + 1) * dv] = _rms(outs[h], nw_ref[...]) * _silu(hg_ref[:, h * dv:(h + 1) * dv])
    _store_state(so_ref, st_ref)


def hgrn_mix(proj, lb, norm_w, s0, *, batch, seq, valid_len=REC_CHUNK, heads_per_step=8):
    H, dk, dv = s0.shape[1:]
    C = REC_CHUNK
    hb = heads_per_step
    assert H % hb == 0
    n = seq // C
    wmat, level = _rec_constants()
    row = lambda b, h, c: b * n + c
    col_spec = lambda grp: pl.BlockSpec((C, hb * dk), lambda b, h, c: (row(b, h, c), grp * (H // hb) + h))
    const = lambda shape: pl.BlockSpec(shape, lambda b, h, c: (0,) * len(shape))
    state_spec = pl.BlockSpec((None, hb, dk, dv), lambda b, h, c: (b, h, 0, 0))
    return pl.pallas_call(
        functools.partial(_hgrn_body, valid_len=valid_len),
        grid=(batch, H // hb, n),
        in_specs=[col_spec(0), col_spec(1), col_spec(2), col_spec(3),
                  pl.BlockSpec((1, hb * dk), lambda b, h, c: (0, h)),
                  const((1, dv)), state_spec, const(wmat.shape), const(level.shape)],
        out_specs=[pl.BlockSpec((C, hb * dv), lambda b, h, c: (row(b, h, c), h)), state_spec],
        out_shape=[jax.ShapeDtypeStruct((batch * n * C, H * dv), F32), jax.ShapeDtypeStruct(s0.shape, F32)],
        scratch_shapes=[pltpu.VMEM((hb, dv, dk), F32)],
        compiler_params=_params("parallel", "parallel", "arbitrary"),
        name="hgrn_mix",
    )(proj, proj, proj, proj, lb.reshape(1, H * dk), norm_w.reshape(1, dv), s0, wmat, level)


def _gla_body(q_ref, k_ref, v_ref, r_ref, a_ref, wg_ref, bg_ref, nw_ref, s0_ref, w_ref, lvl_ref, o_ref, so_ref,
              st_ref, *, valid_len):
    _load_state(s0_ref, st_ref)
    heads, dv, dk = st_ref.shape
    x = _bdot(a_ref[...], wg_ref[...]) + bg_ref[...]
    g = -(jnp.maximum(-x, 0.0) + jnp.log(1.0 + jnp.exp(-jnp.abs(x)))) / GLA_GATE_NORM
    k = k_ref[...]
    if valid_len < REC_CHUNK:
        ok = _valid_rows(g.shape, valid_len)
        k = jnp.where(ok, k, 0.0)
        g = jnp.where(ok, g, 0.0)
    q = q_ref[...] * (dk ** -0.5)
    hs = range(heads)
    outs = _rec_core(q, k, [v_ref[:, h * dv:(h + 1) * dv] for h in hs], g, w_ref, lvl_ref[...], st_ref)
    for h in hs:
        o_ref[:, h * dv:(h + 1) * dv] = _rms(outs[h], nw_ref[...]) * _silu(r_ref[:, h * dv:(h + 1) * dv])
    _store_state(so_ref, st_ref)


def gla_mix(proj, w_gate, layer, b_gate, norm_w, s0, *, batch, seq, valid_len=REC_CHUNK, heads_per_step=4):
    H, dk, dv = s0.shape[1:]
    C = REC_CHUNK
    hb = heads_per_step
    assert H % hb == 0
    n = seq // C
    nh = H // hb
    wmat, level = _rec_constants()
    row = lambda b, h, c: b * n + c
    const = lambda shape: pl.BlockSpec(shape, lambda b, h, c: (0,) * len(shape))
    state_spec = pl.BlockSpec((None, hb, dk, dv), lambda b, h, c: (b, h, 0, 0))
    kv_off = 2 * H * dk // (hb * dv)
    a_blk = (2 * H * dk + 2 * H * dv) // LANES
    return pl.pallas_call(
        functools.partial(_gla_body, valid_len=valid_len),
        grid=(batch, nh, n),
        in_specs=[pl.BlockSpec((C, hb * dk), lambda b, h, c: (row(b, h, c), h)),
                  pl.BlockSpec((C, hb * dk), lambda b, h, c: (row(b, h, c), nh + h)),
                  pl.BlockSpec((C, hb * dv), lambda b, h, c: (row(b, h, c), kv_off + h)),
                  pl.BlockSpec((C, hb * dv), lambda b, h, c: (row(b, h, c), kv_off + nh + h)),
                  pl.BlockSpec((C, LANES), lambda b, h, c: (row(b, h, c), a_blk)),
                  pl.BlockSpec((None, LANES, hb * dk), lambda b, h, c: (layer, 0, h)),
                  pl.BlockSpec((1, hb * dk), lambda b, h, c: (0, h)),
                  const((1, dv)), state_spec, const(wmat.shape), const(level.shape)],
        out_specs=[pl.BlockSpec((C, hb * dv), lambda b, h, c: (row(b, h, c), h)), state_spec],
        out_shape=[jax.ShapeDtypeStruct((batch * n * C, H * dv), F32), jax.ShapeDtypeStruct(s0.shape, F32)],
        scratch_shapes=[pltpu.VMEM((hb, dv, dk), F32)],
        compiler_params=_params("parallel", "parallel", "arbitrary"),
        name="gla_mix",
    )(proj, proj, proj, proj, proj, w_gate, b_gate.reshape(1, H * dk), norm_w.reshape(1, dv), s0, wmat, level)


def _top_blocks(s, index, axis):
    sel = jnp.zeros(s.shape, F32)
    for _ in range(MOBA_TOPK):
        m = jnp.max(s, axis=axis, keepdims=True)
        idx = jnp.min(jnp.where(s == m, index, s.shape[axis]), axis=axis, keepdims=True)
        hit = index == idx
        sel = jnp.where(hit & (m > NEG_INF), 1.0, sel)
        s = jnp.where(hit, NEG_INF, s)
    return sel


def _moba_prompt_body(q_ref, k_ref, v_ref, o_ref, kmean_ref, kaug_ref, vt_ref, lg_ref):
    i = pl.program_id(2)
    blk = MOBA_BLOCK
    tile = MOBA_KEY_TILE * blk
    T = k_ref.shape[0]
    nb = T // blk
    heads, nbp, dh = kmean_ref.shape
    hs = range(heads)
    cols = lambda h: slice(h * dh, (h + 1) * dh)

    @pl.when(i == 0)
    def _():
        kmean_ref[...] = jnp.zeros_like(kmean_ref)
        key_blk = lax.broadcasted_iota(jnp.int32, (T, LANES), 0) // blk
        on_blk = lax.broadcasted_iota(jnp.int32, (T, LANES), 1) == key_blk
        for h in hs:
            for jb in range(nb):
                rows = slice(jb * blk, (jb + 1) * blk)
                kmean_ref[h, jb:jb + 1, :] = jnp.mean(k_ref[rows, cols(h)], axis=0, keepdims=True)
                vt_ref[h, :, rows] = v_ref[rows, cols(h)].T.astype(BF16)
            kaug_ref[h, :, 0:dh] = k_ref[:, cols(h)].astype(BF16)
            kaug_ref[h, :, dh:dh + LANES] = jnp.where(on_blk, MASKED, 0.0).astype(BF16)

    own = pl.multiple_of(i * blk, blk)
    blk_id = lax.broadcasted_iota(jnp.int32, (nbp, blk), 0)
    visible = lax.broadcasted_iota(jnp.int32, (blk, blk), 0) <= lax.broadcasted_iota(jnp.int32, (blk, blk), 1)
    qaugs, m0 = [], []
    for h in hs:
        q = q_ref[:, cols(h)]
        s = lax.dot_general(kmean_ref[h], q, NT_DIMS, precision=lax.Precision.HIGHEST, preferred_element_type=F32)
        sel = _top_blocks(jnp.where(blk_id < i, s, NEG_INF), blk_id, 0)
        qst = (q * (dh ** -0.5)).T.astype(BF16)
        unpicked = jnp.concatenate([1.0 - sel, jnp.zeros((LANES - nbp, blk), F32)], axis=0).astype(BF16)
        qaugs.append(jnp.concatenate([qst, unpicked], axis=0))
        lg_own = jnp.dot(kaug_ref[h, pl.ds(own, blk), 0:dh], qst, preferred_element_type=F32)
        lg_own = jnp.where(visible, lg_own, NEG_INF)
        lg_ref[h, T:T + blk, :] = lg_own
        m0.append(jnp.max(lg_own, axis=0, keepdims=True))
    n_tiles = (i + MOBA_KEY_TILE - 1) // MOBA_KEY_TILE

    def score(t, ms):
        start = pl.multiple_of(t * tile, tile)
        out = []
        for h in hs:
            lg = jnp.dot(kaug_ref[h, pl.ds(start, tile), :], qaugs[h], preferred_element_type=F32)
            lg_ref[h, pl.ds(start, tile), :] = lg
            out.append(jnp.maximum(ms[h], jnp.max(lg, axis=0, keepdims=True)))
        return tuple(out)

    n_pairs = n_tiles // 2
    pair = lambda body: (lambda tp, carry: body(2 * tp + 1, body(2 * tp, carry)))
    ms = lax.fori_loop(0, n_pairs, pair(score), tuple(m0))
    ms = lax.fori_loop(2 * n_pairs, n_tiles, score, ms)

    init = []
    for h in hs:
        p_own = jnp.exp(lg_ref[h, T:T + blk, :] - ms[h])
        acc0 = jnp.dot(vt_ref[h, :, pl.ds(own, blk)], p_own.astype(BF16), preferred_element_type=F32)
        init.append((jnp.sum(p_own, axis=0, keepdims=True), acc0))

    def gather(t, carry):
        start = pl.multiple_of(t * tile, tile)
        out = []
        for h in hs:
            l, acc = carry[h]
            p = jnp.exp(lg_ref[h, pl.ds(start, tile), :] - ms[h])
            acc = acc + jnp.dot(vt_ref[h, :, pl.ds(start, tile)], p.astype(BF16), preferred_element_type=F32)
            out.append((l + jnp.sum(p, axis=0, keepdims=True), acc))
        return tuple(out)

    carry = lax.fori_loop(0, n_pairs, pair(gather), tuple(init))
    carry = lax.fori_loop(2 * n_pairs, n_tiles, gather, carry)
    for h in hs:
        l, acc = carry[h]
        o_ref[:, cols(h)] = (acc / l).T


def moba_prompt(proj, *, batch, seq, q_col, k_col, v_col, heads_per_step=2):
    H, dh, blk = MOBA_HEADS, HEAD_DIM, MOBA_BLOCK
    T = seq
    hb = heads_per_step
    assert T % (MOBA_KEY_TILE * blk) == 0 and T // blk <= LANES
    assert H % hb == 0 and q_col % hb == 0 and k_col % hb == 0 and v_col % hb == 0
    nq = T // blk
    nbp = -(-nq // 8) * 8
    return pl.pallas_call(
        _moba_prompt_body,
        grid=(batch, H // hb, nq),
        in_specs=[pl.BlockSpec((blk, hb * dh), lambda b, h, i: (b * nq + i, q_col // hb + h)),
                  pl.BlockSpec((T, hb * dh), lambda b, h, i: (b, k_col // hb + h)),
                  pl.BlockSpec((T, hb * dh), lambda b, h, i: (b, v_col // hb + h))],
        out_specs=pl.BlockSpec((blk, hb * dh), lambda b, h, i: (b * nq + i, h)),
        out_shape=jax.ShapeDtypeStruct((batch * T, H * dh), F32),
        scratch_shapes=[pltpu.VMEM((hb, nbp, dh), F32), pltpu.VMEM((hb, T, dh + LANES), BF16),
                        pltpu.VMEM((hb, dh, T), BF16), pltpu.VMEM((hb, T + blk, blk), F32)],
        compiler_params=_params("parallel", "parallel", "arbitrary"),
        name="moba_prompt",
    )(proj, proj, proj)


def _stack_heads(q):
    heads = q.shape[1] // HEAD_DIM
    return jnp.concatenate([q[:, h * HEAD_DIM:(h + 1) * HEAD_DIM] for h in range(heads)], axis=0)


def _same_head(rows, cols, lq, heads):
    r = lax.broadcasted_iota(jnp.int32, (rows, cols), 0)
    c = lax.broadcasted_iota(jnp.int32, (rows, cols), 1)
    return (r // lq) == (c % heads)


def _moba_pages_body(pt_ref, q_ref, *refs):
    group = MOBA_PAGE_GROUP
    k_refs, v_refs = refs[:group], refs[group:2 * group]
    sc_ref, m_ref, l_ref, o_ref = refs[2 * group:]
    step = pl.program_id(1)
    lq = q_ref.shape[0]
    n_tok, heads, dh = k_refs[0].shape

    @pl.when(step == 0)
    def _():
        sc_ref[...] = jnp.zeros_like(sc_ref)
        m_ref[...] = jnp.zeros_like(m_ref)
        l_ref[...] = jnp.zeros_like(l_ref)

    qa = _stack_heads(q_ref[...])
    rows = qa.shape[0]
    qs = (qa * (dh ** -0.5)).astype(BF16)
    same = _same_head(rows, n_tok * heads, lq, heads)
    lane = lax.broadcasted_iota(jnp.int32, sc_ref.shape, 1)
    sc_all, m_all, l_all = sc_ref[...], m_ref[...], l_ref[...]
    pages = range(group)
    logits = [lax.dot_general(qs, k_refs[gi][...].reshape(n_tok * heads, dh).astype(BF16), NT_DIMS,
                              preferred_element_type=F32) for gi in pages]
    logits = [jnp.where(same, lg, NEG_INF) for lg in logits]
    ms = [jnp.max(lg, axis=1, keepdims=True) for lg in logits]
    prs = [jnp.exp(logits[gi] - ms[gi]) for gi in pages]
    for gi in pages:
        o_ref[gi] = _bdot(prs[gi], v_refs[gi][...].reshape(n_tok * heads, dh))
    for gi in pages:
        ksum = jnp.sum(k_refs[gi][...], axis=0)
        ksum_rows = jnp.concatenate([jnp.broadcast_to(ksum[h:h + 1, :], (lq, dh)) for h in range(heads)], axis=0)
        here = lane == step * group + gi
        sc_all = jnp.where(here, jnp.sum(qa * ksum_rows, axis=1, keepdims=True), sc_all)
        m_all = jnp.where(here, ms[gi], m_all)
        l_all = jnp.where(here, jnp.sum(prs[gi], axis=1, keepdims=True), l_all)
    sc_ref[...] = sc_all
    m_ref[...] = m_all
    l_ref[...] = l_all


def moba_pages(q, cache_k, cache_v, page_table, layer):
    B, lq, width = q.shape
    _, _, n_tok, heads, dh = cache_k.shape
    n_pages = page_table.shape[1]
    group = MOBA_PAGE_GROUP
    assert n_pages == LANES and heads * dh == width and n_pages % group == 0
    rows = heads * lq
    page_spec = lambda gi: pl.BlockSpec((None, None, n_tok, heads, dh),
                                        lambda b, p, pt: (layer, pt[b, p * group + gi], 0, 0, 0))
    stat_spec = pl.BlockSpec((None, rows, LANES), lambda b, p, pt: (b, 0, 0))
    stat = jax.ShapeDtypeStruct((B, rows, LANES), F32)
    page_specs = [page_spec(gi) for gi in range(group)]
    return pl.pallas_call(
        _moba_pages_body,
        grid_spec=pltpu.PrefetchScalarGridSpec(
            num_scalar_prefetch=1,
            grid=(B, n_pages // group),
            in_specs=[pl.BlockSpec((None, lq, width), lambda b, p, pt: (b, 0, 0))] + page_specs + page_specs,
            out_specs=[stat_spec, stat_spec, stat_spec,
                       pl.BlockSpec((None, group, rows, dh), lambda b, p, pt: (b, p, 0, 0))],
        ),
        out_shape=[stat, stat, stat, jax.ShapeDtypeStruct((B, n_pages, rows, dh), F32)],
        compiler_params=_params("parallel", "arbitrary"),
        name="moba_pages",
    )(page_table, q, *([cache_k] * group), *([cache_v] * group))


def _moba_combine_body(sc_ref, m_ref, l_ref, op_ref, q_ref, kn_ref, vn_ref, o_ref):
    lq = q_ref.shape[0]
    rows = sc_ref.shape[0]
    heads = rows // lq
    n_pages = op_ref.shape[0]
    pages_per_block = MOBA_BLOCK // PAGE_SIZE
    assert pages_per_block == 2
    lane = lax.broadcasted_iota(jnp.int32, (rows, LANES), 1)
    sc = sc_ref[...]
    bs = (sc + pltpu.roll(sc, LANES - 1, 1)) * (1.0 / MOBA_BLOCK)
    sel = _top_blocks(jnp.where(lane % pages_per_block == 0, bs, NEG_INF), lane, 1)
    sel = sel + pltpu.roll(sel, 1, 1)

    qa = _stack_heads(q_ref[...])
    lo = _bdot_nt(qa * (HEAD_DIM ** -0.5), kn_ref[...])
    t_q = lax.broadcasted_iota(jnp.int32, (rows, LANES), 0) % lq
    lo = jnp.where(_same_head(rows, LANES, lq, heads) & (lane // heads <= t_q), lo, NEG_INF)
    m = m_ref[...]
    m_all = jnp.maximum(jnp.max(lo, axis=1, keepdims=True),
                        jnp.max(jnp.where(sel > 0.0, m, NEG_INF), axis=1, keepdims=True))
    w = jnp.where(sel > 0.0, jnp.exp(m - m_all), 0.0)
    p_own = jnp.exp(lo - m_all)
    den = jnp.sum(w * l_ref[...], axis=1, keepdims=True) + jnp.sum(p_own, axis=1, keepdims=True)
    num0 = _bdot(p_own, vn_ref[...])

    num = num0
    for p in range(n_pages):
        num = num + w[:, p:p + 1] * op_ref[p]
    out = num / den
    o_ref[...] = jnp.concatenate([out[h * lq:(h + 1) * lq, :] for h in range(heads)], axis=1)


def moba_combine(sc, m, l, o_pages, q, k_new, v_new):
    B, lq, width = q.shape
    heads, dh = k_new.shape[2:]
    rows = sc.shape[1]
    n_pages = o_pages.shape[1]
    assert LANES % heads == 0 and lq * heads <= LANES
    pad = ((0, 0), (0, LANES // heads - lq), (0, 0), (0, 0))
    kn = jnp.pad(k_new, pad).reshape(B, LANES, dh)
    vn = jnp.pad(v_new, pad).reshape(B, LANES, dh)
    stat_spec = pl.BlockSpec((None, rows, LANES), lambda b: (b, 0, 0))
    new_spec = pl.BlockSpec((None, LANES, dh), lambda b: (b, 0, 0))
    q_spec = pl.BlockSpec((None, lq, width), lambda b: (b, 0, 0))
    return pl.pallas_call(
        _moba_combine_body,
        grid=(B,),
        in_specs=[stat_spec, stat_spec, stat_spec,
                  pl.BlockSpec((None, n_pages, rows, dh), lambda b: (b, 0, 0, 0)),
                  q_spec, new_spec, new_spec],
        out_specs=q_spec,
        out_shape=jax.ShapeDtypeStruct((B, lq, width), F32),
        compiler_params=_params("parallel"),
        name="moba_combine",
    )(sc, m, l, o_pages, q, kn, vn)


def _pad_tokens(a, batch, length):
    a = a.reshape(batch, -1, a.shape[-1])
    return jnp.pad(a, ((0, 0), (0, length - a.shape[1]), (0, 0))).reshape(batch * length, a.shape[-1])


def _unpad_tokens(a, batch, seq):
    return a.reshape(batch, -1, a.shape[-1])[:, :seq].reshape(batch * seq, a.shape[-1])


def _row_tile(rows, target):
    n = max(1, round(rows / target))
    assert rows % n == 0 and (rows // n) % 8 == 0, rows
    return rows // n


class _Group(NamedTuple):
    batch: int
    seq: int
    row0: int
    hgrn_s0: jax.Array
    gla_s0: jax.Array
    past: Optional[tuple]


def _group_rows(proj, grp):
    if grp.seq % REC_CHUNK == 0:
        assert grp.row0 == 0
        return proj, grp.seq, REC_CHUNK
    own = proj[grp.row0:grp.row0 + grp.batch * grp.seq]
    seq = -(-grp.seq // REC_CHUNK) * REC_CHUNK
    assert seq == REC_CHUNK
    return _pad_tokens(own, grp.batch, seq), seq, grp.seq


def _even_mixer(proj, grp, e, lb, hgrn_norm_w):
    hw = HGRN_HEADS * HEAD_DIM
    mw = MOBA_HEADS * HEAD_DIM
    rows = grp.batch * grp.seq
    rec_in, seq, valid = _group_rows(proj[:, :4 * hw] if grp.seq % REC_CHUNK else proj, grp)
    o_h, s_h = hgrn_mix(rec_in, lb[e], hgrn_norm_w[e], grp.hgrn_s0[e], batch=grp.batch, seq=seq, valid_len=valid)
    o_h = _unpad_tokens(o_h, grp.batch, grp.seq)
    own = proj[grp.row0:grp.row0 + rows]
    k_new = own[:, 4 * hw + mw:4 * hw + 2 * mw]
    v_new = own[:, 4 * hw + 2 * mw:4 * hw + 3 * mw]
    if grp.past is None:
        assert grp.row0 == 0
        first = 4 * hw // HEAD_DIM
        o_m = moba_prompt(proj, batch=grp.batch, seq=grp.seq, q_col=first, k_col=first + MOBA_HEADS,
                          v_col=first + 2 * MOBA_HEADS)
    else:
        cache_k, cache_v, page_table = grp.past
        q = own[:, 4 * hw:4 * hw + mw].reshape(grp.batch, grp.seq, mw)
        sc, m, lsum, o_pages = moba_pages(q, cache_k, cache_v, page_table, e)
        heads = lambda a: a.reshape(grp.batch, grp.seq, MOBA_HEADS, HEAD_DIM)
        o_m = moba_combine(sc, m, lsum, o_pages, q, heads(k_new), heads(v_new)).reshape(rows, mw)
    return o_h, o_m, s_h, k_new, v_new


def _odd_mixer(proj, grp, o, gla_w_gate, gla_b_gate, gla_norm_w):
    rec_in, seq, valid = _group_rows(proj, grp)
    o_g, s_g = gla_mix(rec_in, gla_w_gate, o, gla_b_gate[o], gla_norm_w[o], grp.gla_s0[o], batch=grp.batch, seq=seq,
                       valid_len=valid)
    return _unpad_tokens(o_g, grp.batch, grp.seq), s_g


def _trunk(x, groups, lb, wts):
    (norm_w, ffn_gate, ffn_up, ffn_down, even_w_in, even_w_out, hgrn_norm_w, gla_w_in, gla_w_gate,
     gla_b_gate, gla_norm_w, gla_w_out, final_norm_w) = wts
    depth = norm_w.shape[0]
    tm = _row_tile(x.shape[0], 1024)
    out_tile = lambda grp: _row_tile(grp.batch * grp.seq, 512)
    ks, vs, hs, gs = ([[] for _ in groups] for _ in range(4))
    for l in range(depth):
        x = ffn(x, norm_w[l, 0], ffn_gate, ffn_up, ffn_down, l, 0, tm=tm)
        if l % 2 == 0:
            e = l // 2
            proj = norm_matmul(x, norm_w[l, 1], even_w_in, e, tm=tm, tn=1792)
            for gi, grp in enumerate(groups):
                o_h, o_m, s_h, k_new, v_new = _even_mixer(proj, grp, e, lb, hgrn_norm_w)
                x = out_proj(x, o_h, 0, o_m, 0, even_w_out, e, tm=out_tile(grp), row0=grp.row0)
                ks[gi].append(k_new)
                vs[gi].append(v_new)
                hs[gi].append(s_h)
        else:
            o = l // 2
            proj = norm_matmul(x, norm_w[l, 1], gla_w_in, o, tm=tm, tn=1280)
            for gi, grp in enumerate(groups):
                o_g, s_g = _odd_mixer(proj, grp, o, gla_w_gate, gla_b_gate, gla_norm_w)
                x = out_proj(x, o_g, 0, o_g, 1, gla_w_out, o, tm=out_tile(grp), row0=grp.row0)
                gs[gi].append(s_g)
        x = ffn(x, norm_w[l, 2], ffn_gate, ffn_up, ffn_down, l, 1, final_norm_w if l == depth - 1 else None, tm=tm)
    return x, ks, vs, hs, gs


def kernel(x_prompt, x_sample, cache_k, cache_v, state_hgrn, state_gla, page_table, norm_w, ffn_gate, ffn_up,
           ffn_down, even_w_in, even_w_out, hgrn_lb_logits, hgrn_norm_w, gla_w_in, gla_w_gate_up, gla_b_gate,
           gla_norm_w, gla_w_out, final_norm_w):
    lb = jnp.cumsum(jax.nn.softmax(hgrn_lb_logits.astype(F32), axis=0), axis=0)
    lb = lb - lb[0:1]

    n_odd, d_model, odd_in = gla_w_in.shape
    gate_col = odd_in - GLA_GATE_RANK
    assert gate_col % LANES == 0
    odd_cols = -(-(gate_col + LANES) // (5 * MXU_DIM)) * (5 * MXU_DIM)
    gla_w_in_p = jnp.pad(gla_w_in, ((0, 0), (0, 0), (0, odd_cols - odd_in))).astype(BF16)
    gla_w_gate_p = jnp.pad(gla_w_gate_up, ((0, 0), (0, LANES - GLA_GATE_RANK), (0, 0))).astype(BF16)
    wts = (norm_w, ffn_gate, ffn_up, ffn_down, even_w_in.astype(BF16), even_w_out.astype(BF16), hgrn_norm_w,
           gla_w_in_p, gla_w_gate_p, gla_b_gate, gla_norm_w, gla_w_out.astype(BF16), final_norm_w)

    Bp, Lp, D = x_prompt.shape
    Bs, Ls, _ = x_sample.shape
    n_even = state_hgrn.shape[0]
    prompt = _Group(Bp, Lp, 0, jnp.zeros((n_even, Bp) + state_hgrn.shape[2:], F32),
                    jnp.zeros((n_odd, Bp) + state_gla.shape[2:], F32), None)
    sample = _Group(Bs, Ls, Bp * Lp, state_hgrn, state_gla, (cache_k, cache_v, page_table))
    total = Bp * Lp + Bs * Ls
    x = jnp.pad(x_prompt.reshape(Bp * Lp, D), ((0, Bs * Ls + -total % LANES), (0, 0)))
    x = lax.dynamic_update_slice(x, x_sample.reshape(Bs * Ls, D), (Bp * Lp, 0))
    y, ks, vs, hs, gs = _trunk(x, (prompt, sample), lb, wts)

    pages = lambda a: a.reshape(Bp, Lp // PAGE_SIZE, PAGE_SIZE, MOBA_HEADS, HEAD_DIM)
    rows = lambda a: a.reshape(Bs, Ls, MOBA_HEADS, HEAD_DIM)
    return (y[:Bp * Lp].reshape(Bp, Lp, D), y[Bp * Lp:total].reshape(Bs, Ls, D),
            jnp.stack([pages(a) for a in ks[0]]), jnp.stack([pages(a) for a in vs[0]]),
            jnp.stack(hs[0]), jnp.stack(gs[0]),
            jnp.stack([rows(a) for a in ks[1]]), jnp.stack([rows(a) for a in vs[1]]),
            jnp.stack(hs[1]), jnp.stack(gs[1]))
```
